```python
import math
import jax, jax.numpy as jnp
from jax import lax
import numpy as np

D_MODEL = 2048
BATCH = 2
SEQ = 4096
DEPTH = 4
DEC_BATCH = 8
DEC_SEQ = 4
PAST_LEN = 16384
PAGE_SIZE = 128

N_MIXERS = 4
EPS = 1e-6
D_FF = 4 * D_MODEL
A_CHUNK = 128
A_GROUPS = 8
A_WIDTH = D_MODEL
B_HEADS = 16
B_HEAD_DIM = D_MODEL // B_HEADS
B_BLOCK = 256
B_TOPK = 3
B_QGROUP = 32
C_HEADS = 16
C_DK = 128
C_DV = D_MODEL // C_HEADS
C_CHUNK = 64
D_HEADS = 4
D_DK = D_MODEL // 2 // D_HEADS
D_DV = D_MODEL // D_HEADS
D_CHUNK = 64

N_A = len(range(0, DEPTH, N_MIXERS))
N_B = len(range(1, DEPTH, N_MIXERS))
N_C = len(range(2, DEPTH, N_MIXERS))
N_D = len(range(3, DEPTH, N_MIXERS))

kernel_name = 'hybrid_interleaved_gmlp_moba_hgrn2_mlstm_step'

F32 = jnp.float32


def rms_norm(x, g):
    xf = x.astype(F32)
    y = xf * lax.rsqrt(jnp.mean(xf * xf, axis=-1, keepdims=True) + EPS)
    return (y * g.astype(F32)).astype(x.dtype)


def layer_norm(x, g, b):
    xf = x.astype(F32)
    mu = jnp.mean(xf, axis=-1, keepdims=True)
    xc = xf - mu
    y = xc * lax.rsqrt(jnp.mean(xc * xc, axis=-1, keepdims=True) + EPS)
    return (y * g.astype(F32) + b.astype(F32)).astype(x.dtype)


def sq_relu_mlp(x, w_up, w_down):
    h = jnp.maximum(x @ w_up, 0)
    return (h * h) @ w_down


def _chunks(a, c):
    b, L, h = a.shape[:3]
    a = a.reshape(b, L // c, c, h, *a.shape[3:])
    return jnp.moveaxis(a, (1, 3), (0, 2))


def _unchunk(a):
    nc, b, h, c = a.shape[:4]
    a = jnp.moveaxis(a, (0, 2), (1, 3))
    return a.reshape(b, nc * c, h, *a.shape[4:])


def chunk_mlp_mix(x, w_in, ln_g, ln_b, w_s, b_s, w_out):
    bsz, L, _ = x.shape
    z = jax.nn.gelu(x @ w_in)
    u, v = jnp.split(z, 2, axis=-1)
    v = layer_norm(v, ln_g, ln_b)
    c = A_CHUNK if L % A_CHUNK == 0 else L
    nc = L // c
    tri = jnp.tril(jnp.ones((c, c), bool))
    ws = jnp.where(tri, w_s[:, :c, :c], 0).astype(v.dtype)
    vg = v.reshape(bsz, nc, c, A_GROUPS, A_WIDTH // A_GROUPS)
    s = jnp.einsum('gts,bnsgc->bntgc', ws, vg) + b_s[:, :c].T[None, None, :, :, None].astype(v.dtype)
    out = u * s.reshape(bsz, L, A_WIDTH)
    return out @ w_out, v


def moba_project(x, w_qkv):
    b, L, _ = x.shape
    qkv = (x @ w_qkv).reshape(b, L, 3, B_HEADS, B_HEAD_DIM)
    return qkv[:, :, 0], qkv[:, :, 1], qkv[:, :, 2]


def moba_select(q, means, n_past):
    s = jnp.einsum('bhqd,bhnd->bhqn', q.astype(F32), means.astype(F32))
    s = jnp.where(jnp.arange(means.shape[2]) < n_past, s, -jnp.inf)
    _, idx = lax.top_k(s, B_TOPK)
    valid = jnp.broadcast_to(jnp.arange(B_TOPK) < jnp.minimum(n_past, B_TOPK), idx.shape)
    return idx, valid


def moba_attend(q, qpos, k_sel, v_sel, kpos_sel, sel_valid, k_own, v_own, kpos_own, slopes):
    bsz, h, nq, _ = q.shape
    qf = q.astype(F32) * (B_HEAD_DIM ** -0.5)
    s_sel = jnp.einsum('bhqd,bhqnkd->bhqnk', qf, k_sel.astype(F32))
    s_sel = s_sel - slopes[:, None, None, None] * (qpos[:, None, None] - kpos_sel).astype(F32)
    s_sel = jnp.where(sel_valid[..., None], s_sel, -jnp.inf)
    dist = qpos[:, None] - kpos_own[None, :]
    s_own = jnp.einsum('bhqd,bhkd->bhqk', qf, k_own.astype(F32)) - slopes[:, None, None] * dist.astype(F32)
    s_own = jnp.where(dist >= 0, s_own, -jnp.inf)
    n_sel = B_TOPK * B_BLOCK
    p = jax.nn.softmax(jnp.concatenate([s_sel.reshape(bsz, h, nq, n_sel), s_own], axis=-1), axis=-1)
    o = jnp.einsum('bhqnk,bhqnkd->bhqd', p[..., :n_sel].reshape(s_sel.shape), v_sel.astype(F32)) \
        + jnp.einsum('bhqk,bhkd->bhqd', p[..., n_sel:], v_own.astype(F32))
    return o.astype(q.dtype)


def moba_prompt(q, k, v, slopes):
    bsz, L, h, hd = q.shape
    nblk = -(-L // B_BLOCK)
    pad = nblk * B_BLOCK - L
    kb = jnp.pad(k, ((0, 0), (0, pad), (0, 0), (0, 0))).reshape(bsz, nblk, B_BLOCK, h, hd).transpose(0, 3, 1, 2, 4)
    vb = jnp.pad(v, ((0, 0), (0, pad), (0, 0), (0, 0))).reshape(bsz, nblk, B_BLOCK, h, hd).transpose(0, 3, 1, 2, 4)
    nbs = max(nblk, B_TOPK)
    means = jnp.pad(kb.astype(F32).mean(3), ((0, 0), (0, 0), (0, nbs - nblk), (0, 0)))
    qt = q.transpose(0, 2, 1, 3)
    bi = jnp.arange(bsz)[:, None, None, None]
    hi = jnp.arange(h)[None, :, None, None]
    offs = jnp.arange(B_BLOCK)

    def group(start):
        qg = lax.dynamic_slice_in_dim(qt, start, B_QGROUP, axis=2)
        qpos = start + jnp.arange(B_QGROUP)
        blk = start // B_BLOCK
        idx, valid = moba_select(qg, means, blk)
        idx = jnp.minimum(idx, nblk - 1)
        k_sel = kb[bi, hi, idx]
        v_sel = vb[bi, hi, idx]
        kpos_sel = idx[..., None] * B_BLOCK + offs
        k_own = lax.dynamic_index_in_dim(kb, blk, axis=2, keepdims=False)
        v_own = lax.dynamic_index_in_dim(vb, blk, axis=2, keepdims=False)
        return moba_attend(qg, qpos, k_sel, v_sel, kpos_sel, valid, k_own, v_own, blk * B_BLOCK + offs, slopes)

    o = lax.map(group, jnp.arange(L // B_QGROUP) * B_QGROUP)
    return o.transpose(1, 0, 3, 2, 4).reshape(bsz, L, h * hd)


def moba_sample(q, k_new, v_new, cache_k, cache_v, j, page_table, slopes):
    dbsz, S, h, hd = q.shape
    page = cache_k.shape[2]
    past = page_table.shape[1] * page
    own_blk = past // B_BLOCK
    r = past - own_blk * B_BLOCK
    k_past = cache_k[j, page_table].reshape(dbsz, past, h, hd)
    nbs = max(own_blk, B_TOPK)
    means = k_past[:, :own_blk * B_BLOCK].astype(F32).reshape(dbsz, own_blk, B_BLOCK, h, hd).mean(2)
    means = jnp.pad(means, ((0, 0), (0, nbs - own_blk), (0, 0), (0, 0))).transpose(0, 2, 1, 3)
    qt = q.transpose(0, 2, 1, 3)
    qpos = past + jnp.arange(S)
    idx, valid = moba_select(qt, means, own_blk)
    idx = jnp.minimum(idx, max(own_blk, 1) - 1)
    pos = idx[..., None] * B_BLOCK + jnp.arange(B_BLOCK)
    pos_c = jnp.clip(pos, 0, past - 1)
    bi = jnp.arange(dbsz)[:, None, None, None, None]
    hi = jnp.arange(h)[None, :, None, None, None]
    phys = page_table[bi, pos_c // page]
    off = pos_c % page
    k_sel = cache_k[j, phys, off, hi]
    v_sel = cache_v[j, phys, off, hi]
    own_pos = own_blk * B_BLOCK + jnp.arange(r)
    v_own_c = cache_v[j, page_table[:, own_pos // page], own_pos % page]
    k_own = jnp.concatenate([k_past[:, own_blk * B_BLOCK:].astype(k_new.dtype), k_new], axis=1).transpose(0, 2, 1, 3)
    v_own = jnp.concatenate([v_own_c.astype(v_new.dtype), v_new], axis=1).transpose(0, 2, 1, 3)
    kpos_own = own_blk * B_BLOCK + jnp.arange(r + S)
    o = moba_attend(qt, qpos, k_sel, v_sel, pos, valid, k_own, v_own, kpos_own, slopes)
    return o.transpose(0, 2, 1, 3).reshape(dbsz, S, h * hd)


def hgrn2_recurrence(q, k, i, log_f, s0):
    L = q.shape[1]
    c = math.gcd(L, C_CHUNK)
    tri = jnp.tril(jnp.ones((c, c), bool))

    def step(S, inp):
        qc, kc, ic, gc = inp
        G = jnp.cumsum(gc, axis=2)
        o = jnp.einsum('bhtk,bhkv->bhtv', qc * jnp.exp(G), S)
        decay = jnp.exp(jnp.where(tri[:, :, None], G[:, :, :, None, :] - G[:, :, None, :, :], -jnp.inf))
        att = jnp.einsum('bhtk,bhsk,bhtsk->bhts', qc, kc, decay)
        o = o + jnp.einsum('bhts,bhsv->bhtv', att, ic)
        g_last = G[:, :, -1:]
        S = jnp.exp(g_last[:, :, 0])[..., None] * S + jnp.einsum('bhsk,bhsv->bhkv', kc * jnp.exp(g_last - G), ic)
        return S, o

    xs = tuple(_chunks(a.astype(F32), c) for a in (q, k, i, log_f))
    S, o = lax.scan(step, s0.astype(F32), xs)
    return _unchunk(o), S


def hgrn2_mix(x, w_in, lb, norm_g, w_out, s0):
    b, L, _ = x.shape
    q, f, i, g = jnp.split(x @ w_in, 4, axis=-1)
    q = jax.nn.silu(q)
    log_f = jnp.logaddexp(jnp.log(lb), jnp.log1p(-lb) + jax.nn.log_sigmoid(f.astype(F32)))
    k = -jnp.expm1(log_f)
    shp = (b, L, C_HEADS, C_DK)
    o, s = hgrn2_recurrence(q.reshape(shp), k.reshape(shp), i.reshape(b, L, C_HEADS, C_DV), log_f.reshape(shp), s0)
    o = rms_norm(o, norm_g).reshape(b, L, C_HEADS * C_DV) * jax.nn.silu(g.astype(F32))
    return o.astype(x.dtype) @ w_out, s


def mlstm_recurrence(q, k, v, log_i, log_f, c0, n0, m0):
    L = q.shape[1]
    c = math.gcd(L, D_CHUNK)
    tri = jnp.tril(jnp.ones((c, c), bool))

    def step(carry, inp):
        C, n, m = carry
        qc, kc, vc, lic, lfc = inp
        b = jnp.cumsum(lfc, axis=-1)
        dmat = jnp.where(tri, b[..., :, None] - b[..., None, :] + lic[..., None, :], -jnp.inf)
        inter = b + m[..., None]
        m_t = jnp.maximum(inter, dmat.max(-1))
        w_inter = jnp.exp(inter - m_t)
        wqk = jnp.exp(dmat - m_t[..., None]) * jnp.einsum('bhtd,bhsd->bhts', qc, kc)
        num = w_inter[..., None] * jnp.einsum('bhtk,bhkv->bhtv', qc, C) + jnp.einsum('bhts,bhsv->bhtv', wqk, vc)
        den = w_inter * jnp.einsum('bhtk,bhk->bht', qc, n) + wqk.sum(-1)
        h = num / jnp.maximum(jnp.abs(den), jnp.exp(-m_t))[..., None]
        a = b[..., -1:] - b + lic
        m_new = jnp.maximum(b[..., -1] + m, a.max(-1))
        w_c = jnp.exp(b[..., -1] + m - m_new)
        w_s = jnp.exp(a - m_new[..., None])
        C = w_c[..., None, None] * C + jnp.einsum('bhs,bhsk,bhsv->bhkv', w_s, kc, vc)
        n = w_c[..., None] * n + jnp.einsum('bhs,bhsk->bhk', w_s, kc)
        return (C, n, m_new), h

    xs = tuple(_chunks(a.astype(F32), c) for a in (q, k, v, log_i, log_f))
    (C, n, m), h = lax.scan(step, (c0.astype(F32), n0.astype(F32), m0.astype(F32)), xs)
    return _unchunk(h), C, n, m


def mlstm_mix(x, w_in, b_gates, norm_g, w_out, c0, n0, m0):
    b, L, _ = x.shape
    nq, nv = D_HEADS * D_DK, D_HEADS * D_DV
    q, k, v, o, gates = jnp.split(x @ w_in, [nq, 2 * nq, 2 * nq + nv, 2 * nq + 2 * nv], axis=-1)
    gates = gates.astype(F32) + b_gates.astype(F32)
    log_i = gates[..., :D_HEADS]
    log_f = jax.nn.log_sigmoid(gates[..., D_HEADS:])
    h, C, n, m = mlstm_recurrence(q.reshape(b, L, D_HEADS, D_DK),
                                  k.reshape(b, L, D_HEADS, D_DK) * (D_DK ** -0.5),
                                  v.reshape(b, L, D_HEADS, D_DV), log_i, log_f, c0, n0, m0)
    h = rms_norm(h, norm_g.reshape(D_HEADS, D_DV)).reshape(b, L, nv) * jax.nn.sigmoid(o.astype(F32))
    return h.astype(x.dtype) @ w_out, C, n, m


def setup_inputs(seed: int = 0) -> dict:
    key = jax.random.key(seed)
    keys = iter(jax.random.split(key, 48))

    def w(shape, fan_in):
        return jax.random.normal(next(keys), shape, F32) * fan_in ** -0.5

    def gain(shape):
        return 1.0 + 0.05 * jax.random.normal(next(keys), shape, F32)

    def rnd(shape, scale):
        return scale * jax.random.normal(next(keys), shape, F32)

    n_pages = PAST_LEN // PAGE_SIZE
    n_used = DEC_BATCH * n_pages
    n_pool = n_used + max(1, n_used // 4)
    perm = jax.random.permutation(next(keys), n_pool)
    page_table = perm[:n_used].reshape(DEC_BATCH, n_pages).astype(jnp.int32)
    d_in_width = 2 * D_HEADS * D_DK + 2 * D_HEADS * D_DV + 2 * D_HEADS
    d_b_gates = jnp.concatenate([rnd((N_D, D_HEADS), 0.1), 3.0 + rnd((N_D, D_HEADS), 0.1)], axis=-1)
    return {
        'x_prompt': rnd((BATCH, SEQ, D_MODEL), 1.0),
        'x_sample': rnd((DEC_BATCH, DEC_SEQ, D_MODEL), 1.0),
        'cache_k': rnd((N_B, n_pool, PAGE_SIZE, B_HEADS, B_HEAD_DIM), 1.0),
        'cache_v': rnd((N_B, n_pool, PAGE_SIZE, B_HEADS, B_HEAD_DIM), 1.0),
        'page_table': page_table,
        'state_hgrn': rnd((N_C, DEC_BATCH, C_HEADS, C_DK, C_DV), 0.5),
        'state_mlstm_c': rnd((N_D, DEC_BATCH, D_HEADS, D_DK, D_DV), 0.1),
        'state_mlstm_n': rnd((N_D, DEC_BATCH, D_HEADS, D_DK), 0.1),
        'state_mlstm_m': rnd((N_D, DEC_BATCH, D_HEADS), 0.5),
        'norm_mix': gain((DEPTH, D_MODEL)),
        'norm_ffn': gain((DEPTH, D_MODEL)),
        'norm_final': gain((D_MODEL,)),
        'w_ffn_up': w((DEPTH, D_MODEL, D_FF), D_MODEL),
        'w_ffn_down': w((DEPTH, D_FF, D_MODEL), D_FF),
        'a_w_in': w((N_A, D_MODEL, 2 * A_WIDTH), D_MODEL),
        'a_ln_g': gain((N_A, A_WIDTH)),
        'a_ln_b': rnd((N_A, A_WIDTH), 0.02),
        'a_w_s': w((N_A, A_GROUPS, A_CHUNK, A_CHUNK), A_CHUNK),
        'a_b_s': 1.0 + rnd((N_A, A_GROUPS, A_CHUNK), 0.1),
        'a_w_out': w((N_A, A_WIDTH, D_MODEL), A_WIDTH),
        'b_w_qkv': w((N_B, D_MODEL, 3 * B_HEADS * B_HEAD_DIM), D_MODEL),
        'b_w_out': w((N_B, B_HEADS * B_HEAD_DIM, D_MODEL), B_HEADS * B_HEAD_DIM),
        'c_w_in': w((N_C, D_MODEL, 2 * C_HEADS * C_DK + 2 * C_HEADS * C_DV), D_MODEL),
        'c_lower_bound': rnd((DEPTH, C_HEADS * C_DK), 0.5),
        'c_norm_g': gain((N_C, C_DV)),
        'c_w_out': w((N_C, C_HEADS * C_DV, D_MODEL), C_HEADS * C_DV),
        'd_w_in': w((N_D, D_MODEL, d_in_width), D_MODEL),
        'd_b_gates': d_b_gates,
        'd_norm_g': gain((N_D, D_HEADS * D_DV)),
        'd_w_out': w((N_D, D_HEADS * D_DV, D_MODEL), D_HEADS * D_DV),
    }


def reference(x_prompt, x_sample, cache_k, cache_v, page_table, state_hgrn, state_mlstm_c, state_mlstm_n,
              state_mlstm_m, norm_mix, norm_ffn, norm_final, w_ffn_up, w_ffn_down,
              a_w_in, a_ln_g, a_ln_b, a_w_s, a_b_s, a_w_out, b_w_qkv, b_w_out,
              c_w_in, c_lower_bound, c_norm_g, c_w_out, d_w_in, d_b_gates, d_norm_g, d_w_out):
    slopes = jnp.asarray(2.0 ** (-8.0 * np.arange(1, B_HEADS + 1) / B_HEADS), F32)
    lbs = jax.nn.softmax(c_lower_bound.astype(F32), axis=0)
    lbs = jnp.cumsum(lbs, axis=0) - lbs[0]
    bsz, seq, _ = x_prompt.shape
    dbsz, dseq, _ = x_sample.shape
    xp, xs = x_prompt, x_sample
    a_v_s, k_p, v_p, k_s, v_s = [], [], [], [], []
    hg_p, hg_s = [], []
    mc_p, mn_p, mm_p, mc_s, mn_s, mm_s = [], [], [], [], [], []
    for layer in range(DEPTH):
        kind = layer % N_MIXERS
        j = layer // N_MIXERS
        hp = rms_norm(xp, norm_mix[layer])
        hs = rms_norm(xs, norm_mix[layer])
        if kind == 0:
            yp, _ = chunk_mlp_mix(hp, a_w_in[j], a_ln_g[j], a_ln_b[j], a_w_s[j], a_b_s[j], a_w_out[j])
            ys, vrow = chunk_mlp_mix(hs, a_w_in[j], a_ln_g[j], a_ln_b[j], a_w_s[j], a_b_s[j], a_w_out[j])
            a_v_s.append(vrow)
        elif kind == 1:
            qp, kp, vp = moba_project(hp, b_w_qkv[j])
            yp = moba_prompt(qp, kp, vp, slopes) @ b_w_out[j]
            qs, ks, vs = moba_project(hs, b_w_qkv[j])
            ys = moba_sample(qs, ks, vs, cache_k, cache_v, j, page_table, slopes) @ b_w_out[j]
            k_p.append(kp)
            v_p.append(vp)
            k_s.append(ks)
            v_s.append(vs)
        elif kind == 2:
            s0 = jnp.zeros((bsz, C_HEADS, C_DK, C_DV), F32)
            yp, sp = hgrn2_mix(hp, c_w_in[j], lbs[layer], c_norm_g[j], c_w_out[j], s0)
            ys, ss = hgrn2_mix(hs, c_w_in[j], lbs[layer], c_norm_g[j], c_w_out[j], state_hgrn[j])
            hg_p.append(sp)
            hg_s.append(ss)
        else:
            c0 = jnp.zeros((bsz, D_HEADS, D_DK, D_DV), F32)
            n0 = jnp.zeros((bsz, D_HEADS, D_DK), F32)
            m0 = jnp.zeros((bsz, D_HEADS), F32)
            yp, cp, np_, mp = mlstm_mix(hp, d_w_in[j], d_b_gates[j], d_norm_g[j], d_w_out[j], c0, n0, m0)
            ys, cs, ns, ms = mlstm_mix(hs, d_w_in[j], d_b_gates[j], d_norm_g[j], d_w_out[j],
                                       state_mlstm_c[j], state_mlstm_n[j], state_mlstm_m[j])
            mc_p.append(cp)
            mn_p.append(np_)
            mm_p.append(mp)
            mc_s.append(cs)
            mn_s.append(ns)
            mm_s.append(ms)
        xp = xp + yp.astype(xp.dtype)
        xs = xs + ys.astype(xs.dtype)
        xp = xp + sq_relu_mlp(rms_norm(xp, norm_ffn[layer]), w_ffn_up[layer], w_ffn_down[layer]).astype(xp.dtype)
        xs = xs + sq_relu_mlp(rms_norm(xs, norm_ffn[layer]), w_ffn_up[layer], w_ffn_down[layer]).astype(xs.dtype)
    y_prompt = rms_norm(xp, norm_final)
    y_sample = rms_norm(xs, norm_final)
    chunk_v_sample = jnp.stack(a_v_s)
    k_rows_prompt = jnp.stack(k_p)
    v_rows_prompt = jnp.stack(v_p)
    k_rows_sample = jnp.stack(k_s)
    v_rows_sample = jnp.stack(v_s)
    hgrn_state_prompt = jnp.stack(hg_p)
    hgrn_state_sample = jnp.stack(hg_s)
    mlstm_c_prompt = jnp.stack(mc_p)
    mlstm_n_prompt = jnp.stack(mn_p)
    mlstm_m_prompt = jnp.stack(mm_p)
    mlstm_c_sample = jnp.stack(mc_s)
    mlstm_n_sample = jnp.stack(mn_s)
    mlstm_m_sample = jnp.stack(mm_s)
    return (y_prompt, y_sample, chunk_v_sample, k_rows_prompt, v_rows_prompt, k_rows_sample, v_rows_sample,
            hgrn_state_prompt, hgrn_state_sample, mlstm_c_prompt, mlstm_n_prompt, mlstm_m_prompt,
            mlstm_c_sample, mlstm_n_sample, mlstm_m_sample)
```

```python
import functools

import jax
import jax.numpy as jnp
import numpy as np
from jax import lax
from jax.experimental import pallas as pl
from jax.experimental.pallas import tpu as pltpu

F32 = jnp.float32
BF16 = jnp.bfloat16
EPS = 1e-6
NEG_INF = float("-inf")

A_GROUPS = 8
B_HEADS = 16
B_BLOCK = 256
B_TOPK = 3
C_HEADS = 16
C_DK = 128
D_HEADS = 4
N_MIXERS = 4

LANES = 128
SUBLANES = 8
VMEM_LIMIT_BYTES = 56 * 1024 * 1024
SUB = 16


def _cparams(*sem):
    return pltpu.CompilerParams(dimension_semantics=sem, vmem_limit_bytes=VMEM_LIMIT_BYTES)


def _sigmoid(x):
    return 1.0 / (1.0 + jnp.exp(-x))


def _log_sigmoid(x):
    return jnp.minimum(x, 0.0) - jnp.log(1.0 + jnp.exp(-jnp.abs(x)))


def _split3(x):
    hi = x.astype(BF16)
    r1 = x - hi.astype(F32)
    mid = r1.astype(BF16)
    lo = (r1 - mid.astype(F32)).astype(BF16)
    return hi, mid, lo


def _tri_cumsum(x, block):
    n = x.shape[0]
    r = lax.broadcasted_iota(jnp.int32, (n, n), 0)
    c = lax.broadcasted_iota(jnp.int32, (n, n), 1)
    tri = jnp.where((c <= r) & ((r // block) == (c // block)), 1.0, 0.0).astype(BF16)
    hi, mid, lo = _split3(x)
    out = jnp.dot(tri, lo, preferred_element_type=F32)
    out = out + jnp.dot(tri, mid, preferred_element_type=F32)
    return out + jnp.dot(tri, hi, preferred_element_type=F32)


def _dot_nt(a, b):
    return lax.dot_general(a, b, (((1,), (1,)), ((), ())), preferred_element_type=F32)


def _dot_tn(a, b):
    return lax.dot_general(a, b, (((0,), (0,)), ((), ())), preferred_element_type=F32)


def _dot_nt_f32(a, b):
    a0, a1, a2 = _split3(a)
    b0, b1, b2 = _split3(b)
    out = _dot_nt(a1, b1) + _dot_nt(a0, b2) + _dot_nt(a2, b0)
    out = out + _dot_nt(a0, b1) + _dot_nt(a1, b0)
    return out + _dot_nt(a0, b0)


def _norm_mm_kernel(x_ref, g_ref, w_ref, *rest, epilogue, n_extra, n_out):
    extra = rest[:n_extra]
    outs = rest[n_extra:n_extra + n_out]
    xn_ref = rest[n_extra + n_out]

    @pl.when(pl.program_id(1) == 0)
    def _():
        x = x_ref[...]
        y = x * lax.rsqrt(jnp.mean(x * x, axis=-1, keepdims=True) + EPS)
        xn_ref[...] = (y * g_ref[...]).astype(BF16)

    z = jnp.dot(xn_ref[...], w_ref[...], preferred_element_type=F32)
    res = epilogue(z, *[e[...] for e in extra])
    for o_ref, r in zip(outs, res):
        o_ref[...] = r.astype(o_ref.dtype)


def _norm_matmul(x, g, w, col0, ncols, epilogue, out_dtypes, extras=()):
    m, d = x.shape
    tm = min(m, 512)
    tn = min(ncols, 512)
    assert m % tm == 0 and ncols % tn == 0 and col0 % tn == 0
    jb = col0 // tn
    kern = functools.partial(_norm_mm_kernel, epilogue=epilogue, n_extra=len(extras), n_out=len(out_dtypes))
    return pl.pallas_call(
        kern,
        grid=(m // tm, ncols // tn),
        in_specs=[pl.BlockSpec((tm, d), lambda i, j: (i, 0)),
                  pl.BlockSpec((1, d), lambda i, j: (0, 0)),
                  pl.BlockSpec((d, tn), lambda i, j: (0, j + jb))]
                 + [pl.BlockSpec((1, tn), lambda i, j: (0, j)) for _ in extras],
        out_specs=[pl.BlockSpec((tm, tn), lambda i, j: (i, j)) for _ in out_dtypes],
        out_shape=[jax.ShapeDtypeStruct((m, ncols), dt) for dt in out_dtypes],
        scratch_shapes=[pltpu.VMEM((tm, d), BF16)],
        compiler_params=_cparams("parallel", "arbitrary"),
    )(x, g.reshape(1, d), w, *extras)


def _mm_res_kernel(a_ref, w_ref, r_ref, o_ref):
    o_ref[...] = r_ref[...] + jnp.dot(a_ref[...].astype(BF16), w_ref[...], preferred_element_type=F32)


def _matmul_residual(a, w, res):
    m, k = a.shape
    n = w.shape[1]
    tm = min(m, 512)
    tn = min(n, 1024)
    assert m % tm == 0 and n % tn == 0
    return pl.pallas_call(
        _mm_res_kernel,
        grid=(m // tm, n // tn),
        in_specs=[pl.BlockSpec((tm, k), lambda i, j: (i, 0)),
                  pl.BlockSpec((k, tn), lambda i, j: (0, j)),
                  pl.BlockSpec((tm, tn), lambda i, j: (i, j))],
        out_specs=pl.BlockSpec((tm, tn), lambda i, j: (i, j)),
        out_shape=jax.ShapeDtypeStruct((m, n), F32),
        compiler_params=_cparams("parallel", "parallel"),
    )(a, w, res)


def _ffn_kernel(x_ref, g_ref, wu_ref, wd_ref, o_ref, xn_ref):
    f = pl.program_id(1)

    @pl.when(f == 0)
    def _():
        x = x_ref[...]
        y = x * lax.rsqrt(jnp.mean(x * x, axis=-1, keepdims=True) + EPS)
        xn_ref[...] = (y * g_ref[...]).astype(BF16)
        o_ref[...] = x

    h = jnp.maximum(jnp.dot(xn_ref[...], wu_ref[...], preferred_element_type=F32), 0.0)
    o_ref[...] += jnp.dot((h * h).astype(BF16), wd_ref[...], preferred_element_type=F32)


def _ffn(x, g, w_up, w_down):
    m, d = x.shape
    dff = w_up.shape[1]
    tm = min(m, 512)
    tf = min(dff, 512)
    assert m % tm == 0 and dff % tf == 0
    return pl.pallas_call(
        _ffn_kernel,
        grid=(m // tm, dff // tf),
        in_specs=[pl.BlockSpec((tm, d), lambda i, f: (i, 0)),
                  pl.BlockSpec((1, d), lambda i, f: (0, 0)),
                  pl.BlockSpec((d, tf), lambda i, f: (0, f)),
                  pl.BlockSpec((tf, d), lambda i, f: (f, 0))],
        out_specs=pl.BlockSpec((tm, d), lambda i, f: (i, 0)),
        out_shape=jax.ShapeDtypeStruct((m, d), F32),
        scratch_shapes=[pltpu.VMEM((tm, d), BF16)],
        compiler_params=_cparams("parallel", "arbitrary"),
    )(x, g.reshape(1, d), w_up, w_down)


def _rmsnorm_kernel(x_ref, g_ref, o_ref):
    x = x_ref[...]
    o_ref[...] = x * lax.rsqrt(jnp.mean(x * x, axis=-1, keepdims=True) + EPS) * g_ref[...]


def _rmsnorm(x, g):
    m, d = x.shape
    tm = min(m, 512)
    return pl.pallas_call(
        _rmsnorm_kernel,
        grid=(m // tm,),
        in_specs=[pl.BlockSpec((tm, d), lambda i: (i, 0)), pl.BlockSpec((1, d), lambda i: (0, 0))],
        out_specs=pl.BlockSpec((tm, d), lambda i: (i, 0)),
        out_shape=jax.ShapeDtypeStruct((m, d), F32),
        compiler_params=_cparams("parallel"),
    )(x, g.reshape(1, d))


def _gmlp_gate_kernel(z_ref, lg_ref, lb_ref, ws_ref, bs_ref, a_ref, v_ref, *, c, width):
    gw = width // A_GROUPS
    z = z_ref[0]
    u = z[:, :width]
    vr = z[:, width:]
    mu = jnp.mean(vr, axis=-1, keepdims=True)
    vc = vr - mu
    v = vc * lax.rsqrt(jnp.mean(vc * vc, axis=-1, keepdims=True) + EPS) * lg_ref[...] + lb_ref[...]
    v_ref[0] = v
    row = lax.broadcasted_iota(jnp.int32, (c, c), 0)
    col = lax.broadcasted_iota(jnp.int32, (c, c), 1)
    bs = bs_ref[...]
    for g in range(A_GROUPS):
        ws = jnp.where(col <= row, ws_ref[g], 0.0)
        vg = v[:, g * gw:(g + 1) * gw]
        if c >= 2 * SUBLANES:
            s = jnp.dot(ws.astype(BF16), vg.astype(BF16), preferred_element_type=F32)
        else:
            s = jnp.zeros((c, gw), F32)
            for t in range(c):
                s = s + ws[:, t:t + 1] * vg[t:t + 1, :]
        s = s + bs[:, g:g + 1]
        a_ref[0, :, g * gw:(g + 1) * gw] = (u[:, g * gw:(g + 1) * gw] * s).astype(a_ref.dtype)


def _gmlp_gate(z, ln_g, ln_b, w_s, b_s, c):
    nb, _, w2 = z.shape
    width = w2 // 2
    ws = w_s[:, :c, :c]
    bs_t = b_s[:, :c].T
    kern = functools.partial(_gmlp_gate_kernel, c=c, width=width)
    return pl.pallas_call(
        kern,
        grid=(nb,),
        in_specs=[pl.BlockSpec((1, c, w2), lambda i: (i, 0, 0)),
                  pl.BlockSpec((1, width), lambda i: (0, 0)),
                  pl.BlockSpec((1, width), lambda i: (0, 0)),
                  pl.BlockSpec((A_GROUPS, c, c), lambda i: (0, 0, 0)),
                  pl.BlockSpec((c, A_GROUPS), lambda i: (0, 0))],
        out_specs=[pl.BlockSpec((1, c, width), lambda i: (i, 0, 0)),
                   pl.BlockSpec((1, c, width), lambda i: (i, 0, 0))],
        out_shape=[jax.ShapeDtypeStruct((nb, c, width), F32),
                   jax.ShapeDtypeStruct((nb, c, width), F32)],
        compiler_params=_cparams("parallel"),
    )(z, ln_g.reshape(1, width), ln_b.reshape(1, width), ws, bs_t)


def _moba_prompt_kernel(slope_ref, q_ref, k_ref, v_ref, o_ref, k16_ref, v16_ref, means_ref, *, nblk, hd):
    blk = B_BLOCK
    slope = slope_ref[pl.program_id(1)]
    scale = hd ** -0.5
    nbp = means_ref.shape[0]

    means_ref[...] = jnp.zeros_like(means_ref)
    for n in range(nblk):
        kb = k_ref[0, n * blk:(n + 1) * blk, :]
        means_ref[n:n + 1, :] = jnp.sum(kb, axis=0, keepdims=True) * (1.0 / blk)
        k16_ref[n * blk:(n + 1) * blk, :] = kb.astype(BF16)
        v16_ref[n * blk:(n + 1) * blk, :] = v_ref[0, n * blk:(n + 1) * blk, :].astype(BF16)
    means = means_ref[...]

    r_i = lax.broadcasted_iota(jnp.int32, (blk, blk), 0)
    c_i = lax.broadcasted_iota(jnp.int32, (blk, blk), 1)
    dist0 = (r_i - c_i).astype(F32)
    lane = lax.broadcasted_iota(jnp.int32, (blk, nbp), 1)

    def q_block(qb, carry):
        q0 = pl.multiple_of(qb * blk, blk)
        q = q_ref[0, pl.ds(q0, blk), :]
        sc = jnp.where(lane < qb, _dot_nt_f32(q, means), NEG_INF)
        cnt = jnp.zeros((blk, nbp), F32)
        for n2 in range(nblk):
            cn = sc[:, n2:n2 + 1]
            ahead = (cn > sc) | ((cn == sc) & (lane > n2))
            cnt = cnt + jnp.where(ahead, 1.0, 0.0)
        sel = jnp.where((cnt < B_TOPK) & (lane < qb), 1.0, 0.0)

        qs = (q * scale).astype(BF16)
        s = _dot_nt(qs, k16_ref[pl.ds(q0, blk), :]) - slope * dist0
        s = jnp.where(dist0 >= 0, s, NEG_INF)
        m = jnp.max(s, axis=-1, keepdims=True)
        p = jnp.exp(s - m)
        l = jnp.sum(p, axis=-1, keepdims=True)
        acc = jnp.dot(p.astype(BF16), v16_ref[pl.ds(q0, blk), :], preferred_element_type=F32)

        def kv_block(n, st):
            m, l, acc = st
            k0 = pl.multiple_of(n * blk, blk)
            picked = jnp.sum(jnp.where(lane == n, sel, 0.0), axis=-1, keepdims=True) > 0.5
            off = ((qb - n) * blk).astype(F32)
            s = _dot_nt(qs, k16_ref[pl.ds(k0, blk), :]) - slope * (dist0 + off)
            s = jnp.where(picked, s, NEG_INF)
            m_new = jnp.maximum(m, jnp.max(s, axis=-1, keepdims=True))
            alpha = jnp.exp(m - m_new)
            p = jnp.exp(s - m_new)
            l = alpha * l + jnp.sum(p, axis=-1, keepdims=True)
            acc = alpha * acc + jnp.dot(p.astype(BF16), v16_ref[pl.ds(k0, blk), :], preferred_element_type=F32)
            return m_new, l, acc

        m, l, acc = lax.fori_loop(0, qb, kv_block, (m, l, acc))
        o_ref[0, pl.ds(q0, blk), :] = (acc / l).astype(o_ref.dtype)
        return carry

    lax.fori_loop(0, nblk, q_block, 0)


def _moba_prompt(q, k, v, slopes):
    b, seq, hdm = q.shape
    hd = hdm // B_HEADS
    assert seq % B_BLOCK == 0 and hd == LANES
    nblk = seq // B_BLOCK
    assert nblk <= LANES
    kern = functools.partial(_moba_prompt_kernel, nblk=nblk, hd=hd)
    spec = pl.BlockSpec((1, seq, hd), lambda i, h: (i, 0, h))
    return pl.pallas_call(
        kern,
        grid=(b, B_HEADS),
        in_specs=[pl.BlockSpec(memory_space=pltpu.SMEM), spec, spec, spec],
        out_specs=spec,
        out_shape=jax.ShapeDtypeStruct((b, seq, hdm), BF16),
        scratch_shapes=[pltpu.VMEM((seq, hd), BF16), pltpu.VMEM((seq, hd), BF16),
                        pltpu.VMEM((LANES, hd), F32)],
        compiler_params=_cparams("parallel", "parallel"),
    )(slopes, q, k, v)


def _page_mean_kernel(pt_ref, x_ref, o_ref, *, ppb):
    p = pl.program_id(1)
    s = jnp.sum(x_ref[0], axis=0, keepdims=True) * (1.0 / B_BLOCK)

    @pl.when(p % ppb == 0)
    def _():
        o_ref[0, 0] = s

    @pl.when(p % ppb != 0)
    def _():
        o_ref[0, 0] += s


def _cache_block_means(cache, page_table):
    npool, page, hdm = cache.shape
    db, npg = page_table.shape
    ppb = B_BLOCK // page
    nblk = npg // ppb
    kern = functools.partial(_page_mean_kernel, ppb=ppb)
    out = pl.pallas_call(
        kern,
        grid_spec=pltpu.PrefetchScalarGridSpec(
            num_scalar_prefetch=1,
            grid=(db, npg),
            in_specs=[pl.BlockSpec((1, page, hdm), lambda i, p, pt: (pt[i, p], 0, 0))],
            out_specs=pl.BlockSpec((1, 1, 1, hdm), lambda i, p, pt: (i, p // ppb, 0, 0)),
        ),
        out_shape=jax.ShapeDtypeStruct((db, nblk, 1, hdm), F32),
        compiler_params=_cparams("parallel", "arbitrary"),
    )(page_table, cache)
    return out.reshape(db, nblk, hdm)


def _moba_select_kernel(q_ref, mean_ref, o_ref, *, s_len, hd, n_valid):
    rows, hdm = q_ref.shape[1], q_ref.shape[2]
    nblk = mean_ref.shape[1]
    r_i = lax.broadcasted_iota(jnp.int32, (rows, hdm), 0)
    c_i = lax.broadcasted_iota(jnp.int32, (rows, hdm), 1)
    qx = jnp.where((c_i // hd) == (r_i // s_len), q_ref[0], 0.0)
    sc = _dot_nt_f32(qx, mean_ref[0])
    lane = lax.broadcasted_iota(jnp.int32, (rows, nblk), 1).astype(F32)
    out_lane = lax.broadcasted_iota(jnp.int32, (rows, LANES), 1)
    out = jnp.zeros((rows, LANES), F32)
    for t in range(n_valid):
        mx = jnp.max(sc, axis=-1, keepdims=True)
        idx = jnp.min(jnp.where(sc == mx, lane, float(nblk)), axis=-1, keepdims=True)
        out = jnp.where(out_lane == t, idx, out)
        sc = jnp.where(lane == idx, NEG_INF, sc)
    o_ref[0] = out.astype(jnp.int32)


def _moba_select(q_rep, means, s_len, n_valid):
    db, rows, hdm = q_rep.shape
    nblk = means.shape[1]
    kern = functools.partial(_moba_select_kernel, s_len=s_len, hd=hdm // B_HEADS, n_valid=n_valid)
    return pl.pallas_call(
        kern,
        grid=(db,),
        in_specs=[pl.BlockSpec((1, rows, hdm), lambda i: (i, 0, 0)),
                  pl.BlockSpec((1, nblk, hdm), lambda i: (i, 0, 0))],
        out_specs=pl.BlockSpec((1, rows, LANES), lambda i: (i, 0, 0)),
        out_shape=jax.ShapeDtypeStruct((db, rows, LANES), jnp.int32),
        compiler_params=_cparams("parallel"),
    )(q_rep, means)


def _moba_sample_kernel(phys_ref, idx_ref, slope_ref, q_ref, kn_ref, vn_ref, k0_ref, k1_ref, v0_ref, v1_ref,
                        o_ref, q8_ref, m_ref, l_ref, acc_ref, *, s_len, hd, past, n_valid, nsel):
    b, h, j = pl.program_id(0), pl.program_id(1), pl.program_id(2)
    slope = slope_ref[h]
    scale = hd ** -0.5
    page = k0_ref.shape[1]
    row8 = lax.broadcasted_iota(jnp.int32, (SUBLANES, 1), 0)

    @pl.when(j == 0)
    def _():
        q8_ref[...] = jnp.zeros_like(q8_ref)
        q8_ref[0:s_len, :] = q_ref[0] * scale
        q8 = q8_ref[...]
        kn = kn_ref[0]
        vn = vn_ref[0]
        s_cols = []
        for c in range(s_len):
            sc = jnp.sum(q8 * kn[c:c + 1, :], axis=-1, keepdims=True) - slope * (row8 - c).astype(F32)
            s_cols.append(jnp.where(row8 >= c, sc, NEG_INF))
        m = s_cols[0]
        for c in range(1, s_len):
            m = jnp.maximum(m, s_cols[c])
        l = jnp.zeros((SUBLANES, 1), F32)
        acc = jnp.zeros((SUBLANES, hd), F32)
        for c in range(s_len):
            p = jnp.exp(s_cols[c] - m)
            l = l + p
            acc = acc + p * vn[c:c + 1, :]
        m_ref[...] = jnp.broadcast_to(m, m_ref.shape)
        l_ref[...] = jnp.broadcast_to(l, l_ref.shape)
        acc_ref[...] = acc

    sq = j // B_TOPK
    t = j % B_TOPK
    blk_id = idx_ref[(b * B_HEADS + h) * nsel + j]
    qs = q8_ref[...].astype(BF16)
    kb = jnp.concatenate([k0_ref[0], k1_ref[0]], axis=0).astype(BF16)
    vb = jnp.concatenate([v0_ref[0], v1_ref[0]], axis=0).astype(BF16)
    col = lax.broadcasted_iota(jnp.int32, (SUBLANES, 2 * page), 1)
    row = lax.broadcasted_iota(jnp.int32, (SUBLANES, 2 * page), 0)
    dist = (past + row - blk_id * B_BLOCK - col).astype(F32)
    s = _dot_nt(qs, kb) - slope * dist
    live_row = jnp.where(t < n_valid, sq, -1)
    s = jnp.where(row == live_row, s, NEG_INF)
    m = m_ref[:, 0:1]
    l = l_ref[:, 0:1]
    m_new = jnp.maximum(m, jnp.max(s, axis=-1, keepdims=True))
    alpha = jnp.exp(m - m_new)
    p = jnp.exp(s - m_new)
    l = alpha * l + jnp.sum(p, axis=-1, keepdims=True)
    acc = alpha * acc_ref[...] + jnp.dot(p.astype(BF16), vb, preferred_element_type=F32)
    m_ref[...] = jnp.broadcast_to(m_new, m_ref.shape)
    l_ref[...] = jnp.broadcast_to(l, l_ref.shape)
    acc_ref[...] = acc

    @pl.when(j == nsel - 1)
    def _():
        o_ref[0] = (acc / l)[0:s_len, :]


def _moba_sample(q, k_new, v_new, cache_k, cache_v, page_table, slopes):
    db, s_len, hdm = q.shape
    hd = hdm // B_HEADS
    npool, page, _ = cache_k.shape
    npg = page_table.shape[1]
    past = npg * page
    assert B_BLOCK % page == 0 and B_BLOCK // page == 2 and past % B_BLOCK == 0 and s_len <= SUBLANES
    own_blk = past // B_BLOCK
    assert own_blk >= 1
    n_valid = min(own_blk, B_TOPK)
    nsel = s_len * B_TOPK

    means = _cache_block_means(cache_k, page_table)
    q_rep = jnp.tile(q, (1, B_HEADS, 1))
    top = _moba_select(q_rep, means, s_len, n_valid)[:, :, :B_TOPK]
    top = jnp.minimum(top, own_blk - 1)
    idx = top.reshape(db, B_HEADS, nsel)
    ppb = B_BLOCK // page
    pages = idx[..., None] * ppb + jnp.arange(ppb, dtype=jnp.int32)
    phys = jnp.take_along_axis(page_table[:, None, :], pages.reshape(db, 1, -1), axis=2)
    phys = phys.reshape(-1).astype(jnp.int32)
    idx_flat = idx.reshape(-1).astype(jnp.int32)

    def page_spec(pg):
        return pl.BlockSpec((1, page, hd),
                            lambda i, h, j, ph, ix: (ph[((i * B_HEADS + h) * nsel + j) * ppb + pg], 0, h))

    row_spec = pl.BlockSpec((1, s_len, hd), lambda i, h, j, ph, ix: (i, 0, h))
    kern = functools.partial(_moba_sample_kernel, s_len=s_len, hd=hd, past=past, n_valid=n_valid, nsel=nsel)
    return pl.pallas_call(
        kern,
        grid_spec=pltpu.PrefetchScalarGridSpec(
            num_scalar_prefetch=2,
            grid=(db, B_HEADS, nsel),
            in_specs=[pl.BlockSpec(memory_space=pltpu.SMEM), row_spec, row_spec, row_spec,
                      page_spec(0), page_spec(1), page_spec(0), page_spec(1)],
            out_specs=row_spec,
            scratch_shapes=[pltpu.VMEM((SUBLANES, hd), F32), pltpu.VMEM((SUBLANES, LANES), F32),
                            pltpu.VMEM((SUBLANES, LANES), F32), pltpu.VMEM((SUBLANES, hd), F32)],
        ),
        out_shape=jax.ShapeDtypeStruct((db, s_len, hdm), F32),
        compiler_params=_cparams("parallel", "parallel", "arbitrary"),
    )(phys, idx_flat, slopes, q, k_new, v_new, cache_k, cache_k, cache_v, cache_v)


def _hgrn_kernel(q_ref, k_ref, i_ref, f_ref, g_ref, ng_ref, s0_ref, y_ref, s_ref, st_ref, gc_ref, o_ref,
                 *, t_step, cs):
    c = pl.program_id(2)
    nsub = cs // SUB

    @pl.when(c == 0)
    def _():
        st_ref[...] = s0_ref[0, 0].T

    gc_ref[...] = _tri_cumsum(f_ref[0], cs)
    rsub = lax.broadcasted_iota(jnp.int32, (SUB, 1), 0)

    def chunk(ci, carry):
        r0 = pl.multiple_of(ci * cs, cs)
        q = q_ref[0, pl.ds(r0, cs), :]
        k = k_ref[0, pl.ds(r0, cs), :]
        iv = i_ref[0, pl.ds(r0, cs), :]
        gcum = gc_ref[pl.ds(r0, cs), :]
        st = st_ref[...]
        st16 = st.astype(BF16)
        iv16 = iv.astype(BF16)
        o_inter = _dot_nt((q * jnp.exp(gcum)).astype(BF16), st16)
        for si in range(nsub):
            lo, hi = si * SUB, (si + 1) * SUB
            g_i = gcum[lo:hi]
            q_i = q[lo:hi]
            k_i = k[lo:hi]
            i_i = iv[lo:hi]
            acc = o_inter[lo:hi]
            if si > 0:
                gref = gcum[lo - 1:lo]
                qt = (q_i * jnp.exp(g_i - gref)).astype(BF16)
                kt = (k[:lo] * jnp.exp(gref - gcum[:lo])).astype(BF16)
                att = _dot_nt(qt, kt)
                acc = acc + jnp.dot(att.astype(BF16), iv16[:lo], preferred_element_type=F32)
            for s in range(SUB):
                e = jnp.exp(jnp.where(rsub >= s, g_i - g_i[s:s + 1], NEG_INF))
                a = jnp.sum(q_i * k_i[s:s + 1] * e, axis=-1, keepdims=True)
                acc = acc + a * i_i[s:s + 1]
            o_ref[pl.ds(r0 + lo, SUB), :] = acc
        glast = gcum[cs - 1:cs]
        kh = (k * jnp.exp(glast - gcum)).astype(BF16)
        st_ref[...] = st * jnp.exp(glast) + _dot_tn(iv16, kh)
        return carry

    lax.fori_loop(0, t_step // cs, chunk, 0)
    o = o_ref[...]
    y = o * lax.rsqrt(jnp.mean(o * o, axis=-1, keepdims=True) + EPS) * ng_ref[...]
    y_ref[0] = (y * g_ref[0]).astype(y_ref.dtype)

    @pl.when(c == pl.num_programs(2) - 1)
    def _():
        s_ref[0, 0] = st_ref[...].T


def _hgrn_recurrence(q, k, iv, logf, gate, norm_g, s0, t_step, cs):
    b, seq, hdm = q.shape
    dk = C_DK
    dv = hdm // C_HEADS
    assert dk == LANES and dv == LANES and seq % t_step == 0 and t_step % cs == 0 and cs % SUB == 0
    kern = functools.partial(_hgrn_kernel, t_step=t_step, cs=cs)
    tok = pl.BlockSpec((1, t_step, dv), lambda i, h, c: (i, c, h))
    st = pl.BlockSpec((1, 1, dk, dv), lambda i, h, c: (i, h, 0, 0))
    return pl.pallas_call(
        kern,
        grid=(b, C_HEADS, seq // t_step),
        in_specs=[tok, tok, tok, tok, tok, pl.BlockSpec((1, dv), lambda i, h, c: (0, 0)), st],
        out_specs=[tok, st],
        out_shape=[jax.ShapeDtypeStruct((b, seq, hdm), BF16),
                   jax.ShapeDtypeStruct((b, C_HEADS, dk, dv), F32)],
        scratch_shapes=[pltpu.VMEM((dv, dk), F32), pltpu.VMEM((t_step, dk), F32), pltpu.VMEM((t_step, dv), F32)],
        compiler_params=_cparams("parallel", "parallel", "arbitrary"),
    )(q, k, iv, logf, gate, norm_g.reshape(1, dv), s0)


def _mlstm_kernel(q_ref, k_ref, v_ref, og_ref, gt_ref, ng_ref, c0_ref, n0_ref, m0_ref,
                  y_ref, c_ref, n_ref, m_ref, *, c, dk, dv, valid_len):
    @pl.when(pl.program_id(1) == 0)
    def _():
        c_ref[...] = c0_ref[...]
        n_ref[...] = n0_ref[...]
        m_ref[...] = m0_ref[...]

    gt = gt_ref[0]
    rowc = lax.broadcasted_iota(jnp.int32, (c, LANES), 0)
    lanec = lax.broadcasted_iota(jnp.int32, (c, LANES), 1)
    gt = jnp.where(rowc < valid_len, gt, jnp.where(lanec < D_HEADS, -1e30, 0.0))
    cum = _tri_cumsum(gt, c)
    gt_t = gt.T
    cum_t = cum.T
    r_i = lax.broadcasted_iota(jnp.int32, (c, c), 0)
    c_i = lax.broadcasted_iota(jnp.int32, (c, c), 1)
    tril = c_i <= r_i
    for h in range(D_HEADS):
        b_col = cum[:, D_HEADS + h:D_HEADS + h + 1]
        b_row = cum_t[D_HEADS + h:D_HEADS + h + 1, :]
        li_col = gt[:, h:h + 1]
        li_row = gt_t[h:h + 1, :]
        m_prev = m_ref[0, h][:, 0:1]
        n_row = n_ref[0, h]
        c_st = c_ref[0, h]
        qh = q_ref[0, :, h * dk:(h + 1) * dk]
        kh = k_ref[0, :, h * dk:(h + 1) * dk]
        vh16 = v_ref[0, :, h * dv:(h + 1) * dv].astype(BF16)
        qh16 = qh.astype(BF16)

        dmat = jnp.where(tril, b_col - b_row + li_row, NEG_INF)
        inter = b_col + m_prev
        m_t = jnp.maximum(inter, jnp.max(dmat, axis=-1, keepdims=True))
        w_inter = jnp.exp(inter - m_t)
        wqk = jnp.exp(dmat - m_t) * _dot_nt(qh16, kh.astype(BF16))
        num = w_inter * jnp.dot(qh16, c_st.astype(BF16), preferred_element_type=F32) \
            + jnp.dot(wqk.astype(BF16), vh16, preferred_element_type=F32)
        den = w_inter * jnp.sum(qh * n_row, axis=-1, keepdims=True) + jnp.sum(wqk, axis=-1, keepdims=True)
        hh = num / jnp.maximum(jnp.abs(den), jnp.exp(-m_t))

        b_last = b_col[c - 1:c, :]
        a_col = b_last - b_col + li_col
        m_new = jnp.maximum(b_last + m_prev, jnp.max(a_col, axis=0, keepdims=True))
        w_c = jnp.exp(b_last + m_prev - m_new)
        w_s = jnp.exp(a_col - m_new)
        ks = w_s * kh
        c_ref[0, h] = w_c * c_st + _dot_tn(ks.astype(BF16), vh16)
        n_ref[0, h] = w_c * n_row + jnp.sum(ks, axis=0, keepdims=True)
        m_ref[0, h] = jnp.broadcast_to(m_new, (1, LANES))

        y = hh * lax.rsqrt(jnp.mean(hh * hh, axis=-1, keepdims=True) + EPS) * ng_ref[:, h * dv:(h + 1) * dv]
        y_ref[0, :, h * dv:(h + 1) * dv] = (y * og_ref[0, :, h * dv:(h + 1) * dv]).astype(y_ref.dtype)


def _mlstm_recurrence(q, k, v, og, gates, norm_g, c0, n0, m0, c, valid_len):
    b, seq, _ = q.shape
    dk = q.shape[2] // D_HEADS
    dv = v.shape[2] // D_HEADS
    assert seq % c == 0 and (seq == c or valid_len == c)
    n0 = n0.reshape(b, D_HEADS, 1, dk)
    m0 = jnp.broadcast_to(m0.reshape(b, D_HEADS, 1, 1), (b, D_HEADS, 1, LANES))
    kern = functools.partial(_mlstm_kernel, c=c, dk=dk, dv=dv, valid_len=valid_len)

    def tok(w):
        return pl.BlockSpec((1, c, w), lambda i, j: (i, j, 0))

    c_spec = pl.BlockSpec((1, D_HEADS, dk, dv), lambda i, j: (i, 0, 0, 0))
    n_spec = pl.BlockSpec((1, D_HEADS, 1, dk), lambda i, j: (i, 0, 0, 0))
    m_spec = pl.BlockSpec((1, D_HEADS, 1, LANES), lambda i, j: (i, 0, 0, 0))
    y, c_out, n_out, m_out = pl.pallas_call(
        kern,
        grid=(b, seq // c),
        in_specs=[tok(D_HEADS * dk), tok(D_HEADS * dk), tok(D_HEADS * dv), tok(D_HEADS * dv), tok(LANES),
                  pl.BlockSpec((1, D_HEADS * dv), lambda i, j: (0, 0)), c_spec, n_spec, m_spec],
        out_specs=[tok(D_HEADS * dv), c_spec, n_spec, m_spec],
        out_shape=[jax.ShapeDtypeStruct((b, seq, D_HEADS * dv), BF16),
                   jax.ShapeDtypeStruct((b, D_HEADS, dk, dv), F32),
                   jax.ShapeDtypeStruct((b, D_HEADS, 1, dk), F32),
                   jax.ShapeDtypeStruct((b, D_HEADS, 1, LANES), F32)],
        compiler_params=_cparams("parallel", "arbitrary"),
    )(q, k, v, og, gates, norm_g.reshape(1, D_HEADS * dv), c0, n0, m0)
    return y, c_out, n_out.reshape(b, D_HEADS, dk), m_out[:, :, 0, 0]


def _ident(z):
    return (z,)


def _gelu(z):
    return (jax.nn.gelu(z, approximate=True),)


def _mix_gmlp(x, g, w_in, ln_g, ln_b, w_s, b_s, w_out):
    b, seq, d = x.shape
    width = w_in.shape[1] // 2
    chunk = w_s.shape[1]
    c = chunk if seq % chunk == 0 else seq
    x2 = x.reshape(b * seq, d)
    (z,) = _norm_matmul(x2, g, w_in, 0, 2 * width, _gelu, (F32,))
    a, v = _gmlp_gate(z.reshape(b * seq // c, c, 2 * width), ln_g, ln_b, w_s, b_s, c)
    y = _matmul_residual(a.reshape(b * seq, width), w_out, x2)
    return y.reshape(b, seq, d), v.reshape(b, seq, width)


def _moba_qkv(x, g, w_qkv):
    b, seq, d = x.shape
    hdm = w_qkv.shape[1] // 3
    x2 = x.reshape(b * seq, d)
    return [_norm_matmul(x2, g, w_qkv, i * hdm, hdm, _ident, (F32,))[0].reshape(b, seq, hdm) for i in range(3)]


def _hgrn_project(x2, g, w_in, lb):
    hdm = w_in.shape[1] // 4
    lb = lb.reshape(1, hdm)

    def silu(z):
        return (z * _sigmoid(z),)

    def forget(z, lbv):
        logf = jnp.log(lbv + (1.0 - lbv) * _sigmoid(z))
        return logf, (1.0 - lbv) * _sigmoid(-z)

    (q,) = _norm_matmul(x2, g, w_in, 0, hdm, silu, (F32,))
    logf, k = _norm_matmul(x2, g, w_in, hdm, hdm, forget, (F32, F32), extras=(lb,))
    (iv,) = _norm_matmul(x2, g, w_in, 2 * hdm, hdm, _ident, (F32,))
    (gate,) = _norm_matmul(x2, g, w_in, 3 * hdm, hdm, silu, (F32,))
    return q, k, iv, logf, gate


def _mix_hgrn(x, g, w_in, lb, norm_g, w_out, s0):
    b, seq, d = x.shape
    x2 = x.reshape(b * seq, d)
    parts = [p.reshape(b, seq, -1) for p in _hgrn_project(x2, g, w_in, lb)]
    if seq % 256 == 0:
        t_step, cs, pad = 256, 64, 0
    else:
        t_step = cs = -(-seq // SUB) * SUB
        pad = t_step - seq
        parts = [jnp.pad(p, ((0, 0), (0, pad), (0, 0))) for p in parts]
    y, s = _hgrn_recurrence(*parts, norm_g, s0, t_step, cs)
    y = y[:, :seq].reshape(b * seq, -1)
    return _matmul_residual(y, w_out, x2).reshape(b, seq, d), s


def _mix_mlstm(x, g, w_in, w_gates, b_gates, norm_g, w_out, c0, n0, m0):
    b, seq, d = x.shape
    dk = c0.shape[2]
    dv = c0.shape[3]
    nq, nv = D_HEADS * dk, D_HEADS * dv
    x2 = x.reshape(b * seq, d)

    def kscale(z):
        return (z * (dk ** -0.5),)

    def ogate(z):
        return (_sigmoid(z),)

    def gates_fn(z, bias):
        zz = z + bias
        lane = lax.broadcasted_iota(jnp.int32, zz.shape, 1)
        return (jnp.where(lane < D_HEADS, zz, _log_sigmoid(zz)),)

    (q,) = _norm_matmul(x2, g, w_in, 0, nq, _ident, (F32,))
    (k,) = _norm_matmul(x2, g, w_in, nq, nq, kscale, (F32,))
    (v,) = _norm_matmul(x2, g, w_in, 2 * nq, nv, _ident, (F32,))
    (og,) = _norm_matmul(x2, g, w_in, 2 * nq + nv, nv, ogate, (F32,))
    (gt,) = _norm_matmul(x2, g, w_gates, 0, LANES, gates_fn, (F32,), extras=(b_gates,))
    parts = [p.reshape(b, seq, -1) for p in (q, k, v, og, gt)]
    if seq % 256 == 0:
        c, valid = 256, 256
    else:
        c = -(-seq // SUB) * SUB
        valid = seq
        parts = [jnp.pad(p, ((0, 0), (0, c - seq), (0, 0))) for p in parts]
    y, c_out, n_out, m_out = _mlstm_recurrence(*parts, norm_g, c0, n0, m0, c, valid)
    y = y[:, :seq].reshape(b * seq, nv)
    return _matmul_residual(y, w_out, x2).reshape(b, seq, d), c_out, n_out, m_out


def kernel(x_prompt, x_sample, cache_k, cache_v, page_table, state_hgrn, state_mlstm_c, state_mlstm_n, state_mlstm_m, norm_mix, norm_ffn, norm_final, w_ffn_up, w_ffn_down, a_w_in, a_ln_g, a_ln_b, a_w_s, a_b_s, a_w_out, b_w_qkv, b_w_out, c_w_in, c_lower_bound, c_norm_g, c_w_out, d_w_in, d_b_gates, d_norm_g, d_w_out):
    depth = norm_mix.shape[0]
    bsz, seq, d = x_prompt.shape
    dbsz, dseq, _ = x_sample.shape
    slopes = jnp.asarray(2.0 ** (-8.0 * np.arange(1, B_HEADS + 1) / B_HEADS), F32)
    lbs = jax.nn.softmax(c_lower_bound.astype(F32), axis=0)
    lbs = jnp.cumsum(lbs, axis=0) - lbs[0]
    hdm = b_w_qkv.shape[2] // 3

    xp, xs = x_prompt, x_sample
    outs = {k: [] for k in ("av", "kp", "vp", "ks", "vs", "hp", "hs", "cp", "np", "mp", "cs", "ns", "ms")}
    for layer in range(depth):
        kind = layer % N_MIXERS
        j = layer // N_MIXERS
        g = norm_mix[layer]
        if kind == 0:
            args = (a_w_in[j].astype(BF16), a_ln_g[j], a_ln_b[j], a_w_s[j], a_b_s[j], a_w_out[j].astype(BF16))
            xp, _ = _mix_gmlp(xp, g, *args)
            xs, vrow = _mix_gmlp(xs, g, *args)
            outs["av"].append(vrow)
        elif kind == 1:
            w_qkv = b_w_qkv[j].astype(BF16)
            w_out = b_w_out[j].astype(BF16)
            qp, kp, vp = _moba_qkv(xp, g, w_qkv)
            op = _moba_prompt(qp, kp, vp, slopes)
            xp = _matmul_residual(op.reshape(bsz * seq, hdm), w_out, xp.reshape(bsz * seq, d)).reshape(bsz, seq, d)
            qs, ks, vs = _moba_qkv(xs, g, w_qkv)
            ck = cache_k[j].reshape(cache_k.shape[1], cache_k.shape[2], hdm)
            cv = cache_v[j].reshape(cache_v.shape[1], cache_v.shape[2], hdm)
            os_ = _moba_sample(qs, ks, vs, ck, cv, page_table, slopes)
            xs = _matmul_residual(os_.reshape(dbsz * dseq, hdm), w_out,
                                  xs.reshape(dbsz * dseq, d)).reshape(dbsz, dseq, d)
            hd = hdm // B_HEADS
            outs["kp"].append(kp.reshape(bsz, seq, B_HEADS, hd))
            outs["vp"].append(vp.reshape(bsz, seq, B_HEADS, hd))
            outs["ks"].append(ks.reshape(dbsz, dseq, B_HEADS, hd))
            outs["vs"].append(vs.reshape(dbsz, dseq, B_HEADS, hd))
        elif kind == 2:
            args = (c_w_in[j].astype(BF16), lbs[layer], c_norm_g[j], c_w_out[j].astype(BF16))
            s0 = jnp.zeros((bsz,) + state_hgrn.shape[2:], F32)
            xp, sp = _mix_hgrn(xp, g, *args, s0)
            xs, ss = _mix_hgrn(xs, g, *args, state_hgrn[j])
            outs["hp"].append(sp)
            outs["hs"].append(ss)
        else:
            ng = 2 * D_HEADS * (state_mlstm_c.shape[3] + state_mlstm_c.shape[4])
            w_in = d_w_in[j]
            w_gates = jnp.pad(w_in[:, ng:], ((0, 0), (0, LANES - 2 * D_HEADS))).astype(BF16)
            b_gates = jnp.pad(d_b_gates[j], (0, LANES - 2 * D_HEADS)).reshape(1, LANES)
            args = (w_in.astype(BF16), w_gates, b_gates, d_norm_g[j], d_w_out[j].astype(BF16))
            zc = jnp.zeros((bsz,) + state_mlstm_c.shape[2:], F32)
            zn = jnp.zeros((bsz,) + state_mlstm_n.shape[2:], F32)
            zm = jnp.zeros((bsz,) + state_mlstm_m.shape[2:], F32)
            xp, cp, np_, mp = _mix_mlstm(xp, g, *args, zc, zn, zm)
            xs, cs, ns, ms = _mix_mlstm(xs, g, *args, state_mlstm_c[j], state_mlstm_n[j], state_mlstm_m[j])
            for key, val in zip(("cp", "np", "mp", "cs", "ns", "ms"), (cp, np_, mp, cs, ns, ms)):
                outs[key].append(val)
        w_up = w_ffn_up[layer].astype(BF16)
        w_down = w_ffn_down[layer].astype(BF16)
        xp = _ffn(xp.reshape(bsz * seq, d), norm_ffn[layer], w_up, w_down).reshape(bsz, seq, d)
        xs = _ffn(xs.reshape(dbsz * dseq, d), norm_ffn[layer], w_up, w_down).reshape(dbsz, dseq, d)
    y_prompt = _rmsnorm(xp.reshape(bsz * seq, d), norm_final).reshape(bsz, seq, d)
    y_sample = _rmsnorm(xs.reshape(dbsz * dseq, d), norm_final).reshape(dbsz, dseq, d)
    st = {k: jnp.stack(v) for k, v in outs.items()}
    return (y_prompt, y_sample, st["av"], st["kp"], st["vp"], st["ks"], st["vs"], st["hp"], st["hs"],
            st["cp"], st["np"], st["mp"], st["cs"], st["ns"], st["ms"])
```

```python
import functools

import jax
import jax.numpy as jnp
import numpy as np
from jax import lax
from jax.experimental import pallas as pl
from jax.experimental.pallas import tpu as pltpu

F32 = jnp.float32
BF16 = jnp.bfloat16
EPS = 1e-6
NEG_INF = float("-inf")

A_GROUPS = 8
B_HEADS = 16
B_BLOCK = 256
B_TOPK = 3
C_HEADS = 16
C_DK = 128
D_HEADS = 4
N_MIXERS = 4

LANES = 128
SUBLANES = 8
VMEM_LIMIT_BYTES = 56 * 1024 * 1024
SUB = 16
MM_TILE_M = 1024
MM_TILE_N = 512
FFN_TILE_F = 512
RES_TILE_N = 1024


def _cparams(*sem):
    return pltpu.CompilerParams(dimension_semantics=sem, vmem_limit_bytes=VMEM_LIMIT_BYTES)


def _sigmoid(x):
    return 1.0 / (1.0 + jnp.exp(-x))


def _log_sigmoid(x):
    return jnp.minimum(x, 0.0) - jnp.log(1.0 + jnp.exp(-jnp.abs(x)))


def _split3(x):
    hi = x.astype(BF16)
    r1 = x - hi.astype(F32)
    mid = r1.astype(BF16)
    lo = (r1 - mid.astype(F32)).astype(BF16)
    return hi, mid, lo


def _tri_cumsum(x, block):
    n = x.shape[0]
    r = lax.broadcasted_iota(jnp.int32, (n, n), 0)
    c = lax.broadcasted_iota(jnp.int32, (n, n), 1)
    tri = jnp.where((c <= r) & ((r // block) == (c // block)), 1.0, 0.0).astype(BF16)
    hi, mid, lo = _split3(x)
    out = jnp.dot(tri, lo, preferred_element_type=F32)
    out = out + jnp.dot(tri, mid, preferred_element_type=F32)
    return out + jnp.dot(tri, hi, preferred_element_type=F32)


def _dot_nt(a, b):
    return lax.dot_general(a, b, (((1,), (1,)), ((), ())), preferred_element_type=F32)


def _dot_tn(a, b):
    return lax.dot_general(a, b, (((0,), (0,)), ((), ())), preferred_element_type=F32)


def _dot_nt_f32(a, b):
    a0, a1, a2 = _split3(a)
    b0, b1, b2 = _split3(b)
    out = _dot_nt(a1, b1) + _dot_nt(a0, b2) + _dot_nt(a2, b0)
    out = out + _dot_nt(a0, b1) + _dot_nt(a1, b0)
    return out + _dot_nt(a0, b0)


def _norm_mm_kernel(x_ref, g_ref, w_ref, *rest, epilogue, n_extra, n_out):
    extra = rest[:n_extra]
    outs = rest[n_extra:n_extra + n_out]
    xn_ref = rest[n_extra + n_out]

    @pl.when(pl.program_id(1) == 0)
    def _():
        x = x_ref[...]
        y = x * lax.rsqrt(jnp.mean(x * x, axis=-1, keepdims=True) + EPS)
        xn_ref[...] = (y * g_ref[...]).astype(BF16)

    z = jnp.dot(xn_ref[...], w_ref[...], preferred_element_type=F32)
    res = epilogue(z, *[e[...] for e in extra])
    for o_ref, r in zip(outs, res):
        o_ref[...] = r.astype(o_ref.dtype)


def _norm_matmul(x, g, w, col0, ncols, epilogue, out_dtypes, extras=()):
    m, d = x.shape
    tm = min(m, MM_TILE_M)
    tn = min(ncols, MM_TILE_N)
    assert m % tm == 0 and ncols % tn == 0 and col0 % tn == 0
    jb = col0 // tn
    kern = functools.partial(_norm_mm_kernel, epilogue=epilogue, n_extra=len(extras), n_out=len(out_dtypes))
    return pl.pallas_call(
        kern,
        grid=(m // tm, ncols // tn),
        in_specs=[pl.BlockSpec((tm, d), lambda i, j: (i, 0)),
                  pl.BlockSpec((1, d), lambda i, j: (0, 0)),
                  pl.BlockSpec((d, tn), lambda i, j: (0, j + jb))]
                 + [pl.BlockSpec((1, tn), lambda i, j: (0, j)) for _ in extras],
        out_specs=[pl.BlockSpec((tm, tn), lambda i, j: (i, j)) for _ in out_dtypes],
        out_shape=[jax.ShapeDtypeStruct((m, ncols), dt) for dt in out_dtypes],
        scratch_shapes=[pltpu.VMEM((tm, d), BF16)],
        compiler_params=_cparams("parallel", "arbitrary"),
        name="norm_matmul",
    )(x, g.reshape(1, d), w, *extras)


def _mm_res_kernel(a_ref, w_ref, r_ref, o_ref):
    o_ref[...] = r_ref[...] + jnp.dot(a_ref[...].astype(BF16), w_ref[...], preferred_element_type=F32)


def _matmul_residual(a, w, res):
    m, k = a.shape
    n = w.shape[1]
    tm = min(m, MM_TILE_M)
    tn = min(n, RES_TILE_N)
    assert m % tm == 0 and n % tn == 0
    return pl.pallas_call(
        _mm_res_kernel,
        grid=(m // tm, n // tn),
        in_specs=[pl.BlockSpec((tm, k), lambda i, j: (i, 0)),
                  pl.BlockSpec((k, tn), lambda i, j: (0, j)),
                  pl.BlockSpec((tm, tn), lambda i, j: (i, j))],
        out_specs=pl.BlockSpec((tm, tn), lambda i, j: (i, j)),
        out_shape=jax.ShapeDtypeStruct((m, n), F32),
        compiler_params=_cparams("parallel", "parallel"),
        name="matmul_residual",
    )(a, w, res)


def _ffn_kernel(x_ref, g_ref, wu_ref, wd_ref, o_ref, xn_ref):
    f = pl.program_id(1)

    @pl.when(f == 0)
    def _():
        x = x_ref[...]
        y = x * lax.rsqrt(jnp.mean(x * x, axis=-1, keepdims=True) + EPS)
        xn_ref[...] = (y * g_ref[...]).astype(BF16)
        o_ref[...] = x

    h = jnp.maximum(jnp.dot(xn_ref[...], wu_ref[...], preferred_element_type=F32), 0.0)
    o_ref[...] += jnp.dot((h * h).astype(BF16), wd_ref[...], preferred_element_type=F32)


def _ffn(x, g, w_up, w_down):
    m, d = x.shape
    dff = w_up.shape[1]
    tm = min(m, MM_TILE_M)
    tf = min(dff, FFN_TILE_F)
    assert m % tm == 0 and dff % tf == 0
    return pl.pallas_call(
        _ffn_kernel,
        grid=(m // tm, dff // tf),
        in_specs=[pl.BlockSpec((tm, d), lambda i, f: (i, 0)),
                  pl.BlockSpec((1, d), lambda i, f: (0, 0)),
                  pl.BlockSpec((d, tf), lambda i, f: (0, f)),
                  pl.BlockSpec((tf, d), lambda i, f: (f, 0))],
        out_specs=pl.BlockSpec((tm, d), lambda i, f: (i, 0)),
        out_shape=jax.ShapeDtypeStruct((m, d), F32),
        scratch_shapes=[pltpu.VMEM((tm, d), BF16)],
        compiler_params=_cparams("parallel", "arbitrary"),
        name="ffn",
    )(x, g.reshape(1, d), w_up, w_down)


def _rmsnorm_kernel(x_ref, g_ref, o_ref):
    x = x_ref[...]
    o_ref[...] = x * lax.rsqrt(jnp.mean(x * x, axis=-1, keepdims=True) + EPS) * g_ref[...]


def _rmsnorm(x, g):
    m, d = x.shape
    tm = min(m, 512)
    return pl.pallas_call(
        _rmsnorm_kernel,
        grid=(m // tm,),
        in_specs=[pl.BlockSpec((tm, d), lambda i: (i, 0)), pl.BlockSpec((1, d), lambda i: (0, 0))],
        out_specs=pl.BlockSpec((tm, d), lambda i: (i, 0)),
        out_shape=jax.ShapeDtypeStruct((m, d), F32),
        compiler_params=_cparams("parallel"),
        name="final_rmsnorm",
    )(x, g.reshape(1, d))


def _gmlp_gate_kernel(z_ref, lg_ref, lb_ref, ws_ref, bs_ref, a_ref, v_ref, *, c, width):
    gw = width // A_GROUPS
    z = z_ref[0]
    u = z[:, :width]
    vr = z[:, width:]
    mu = jnp.mean(vr, axis=-1, keepdims=True)
    vc = vr - mu
    v = vc * lax.rsqrt(jnp.mean(vc * vc, axis=-1, keepdims=True) + EPS) * lg_ref[...] + lb_ref[...]
    v_ref[0] = v
    row = lax.broadcasted_iota(jnp.int32, (c, c), 0)
    col = lax.broadcasted_iota(jnp.int32, (c, c), 1)
    bs = bs_ref[...]
    for g in range(A_GROUPS):
        ws = jnp.where(col <= row, ws_ref[g], 0.0)
        vg = v[:, g * gw:(g + 1) * gw]
        if c >= 2 * SUBLANES:
            s = jnp.dot(ws.astype(BF16), vg.astype(BF16), preferred_element_type=F32)
        else:
            s = jnp.zeros((c, gw), F32)
            for t in range(c):
                s = s + ws[:, t:t + 1] * vg[t:t + 1, :]
        s = s + bs[:, g:g + 1]
        a_ref[0, :, g * gw:(g + 1) * gw] = (u[:, g * gw:(g + 1) * gw] * s).astype(a_ref.dtype)


def _gmlp_gate(z, ln_g, ln_b, w_s, b_s, c):
    nb, _, w2 = z.shape
    width = w2 // 2
    ws = w_s[:, :c, :c]
    bs_t = b_s[:, :c].T
    kern = functools.partial(_gmlp_gate_kernel, c=c, width=width)
    return pl.pallas_call(
        kern,
        grid=(nb,),
        in_specs=[pl.BlockSpec((1, c, w2), lambda i: (i, 0, 0)),
                  pl.BlockSpec((1, width), lambda i: (0, 0)),
                  pl.BlockSpec((1, width), lambda i: (0, 0)),
                  pl.BlockSpec((A_GROUPS, c, c), lambda i: (0, 0, 0)),
                  pl.BlockSpec((c, A_GROUPS), lambda i: (0, 0))],
        out_specs=[pl.BlockSpec((1, c, width), lambda i: (i, 0, 0)),
                   pl.BlockSpec((1, c, width), lambda i: (i, 0, 0))],
        out_shape=[jax.ShapeDtypeStruct((nb, c, width), BF16 if c % (2 * SUBLANES) == 0 else F32),
                   jax.ShapeDtypeStruct((nb, c, width), F32)],
        compiler_params=_cparams("parallel"),
        name="gmlp_gate",
    )(z, ln_g.reshape(1, width), ln_b.reshape(1, width), ws, bs_t)


MASKED = -1e30
V_PAD_ROWS = 16


def _moba_prompt_kernel(slope_ref, q_ref, k_ref, v_ref, o_ref, kaug_ref, vt_ref, means_ref, bias_ref, *, nblk, hd):
    blk = B_BLOCK
    slope = slope_ref[pl.program_id(1)]
    scale = hd ** -0.5
    nbp = means_ref.shape[0]

    lane_blk = lax.broadcasted_iota(jnp.int32, (blk, hd), 1)
    ones_rows = jnp.where(lax.broadcasted_iota(jnp.int32, (V_PAD_ROWS, blk), 0) == 0, 1.0, 0.0).astype(BF16)
    means_ref[...] = jnp.zeros_like(means_ref)
    for n in range(nblk):
        kb = k_ref[0, n * blk:(n + 1) * blk, :]
        means_ref[n:n + 1, :] = jnp.sum(kb, axis=0, keepdims=True) * (1.0 / blk)
        kaug_ref[n * blk:(n + 1) * blk, 0:hd] = kb.astype(BF16)
        kaug_ref[n * blk:(n + 1) * blk, hd:2 * hd] = jnp.where(lane_blk == n, 1.0, 0.0).astype(BF16)
        vt_ref[n, 0:hd, :] = v_ref[0, n * blk:(n + 1) * blk, :].T.astype(BF16)
        vt_ref[n, hd:hd + V_PAD_ROWS, :] = ones_rows
    mean_hi, mean_mid, _ = _split3(means_ref[...])

    key_i = lax.broadcasted_iota(jnp.int32, (blk, blk), 0)
    qry_i = lax.broadcasted_iota(jnp.int32, (blk, blk), 1)
    dist = (qry_i - key_i).astype(F32)
    bias_ref[0:blk, :] = -slope * (dist + blk)
    bias_ref[blk:2 * blk, :] = -slope * dist
    bias_ref[2 * blk:3 * blk, :] = jnp.where(dist >= 0, -slope * dist, NEG_INF)
    rown = lax.broadcasted_iota(jnp.int32, (nbp, blk), 0)

    def q_block(qb, carry):
        q0 = pl.multiple_of(qb * blk, blk)
        q = q_ref[0, pl.ds(q0, blk), :]
        q_hi, q_mid, _ = _split3(q)
        sc = _dot_nt(mean_hi, q_mid) + _dot_nt(mean_mid, q_hi) + _dot_nt(mean_hi, q_hi)
        sc = jnp.where(rown < qb, sc, NEG_INF)
        cnt = jnp.zeros((nbp, blk), F32)
        for n2 in range(nblk):
            rn = sc[n2:n2 + 1, :]
            ahead = (rn > sc) | ((rn == sc) & (rown > n2))
            cnt = cnt + jnp.where(ahead, 1.0, 0.0)
        keep = ((cnt < B_TOPK) & (rown < qb)) | (rown == qb)
        selb = jnp.where(keep, 0.0, MASKED)
        qaug = jnp.concatenate([(q * scale).T, selb, jnp.zeros((hd - nbp, blk), F32)], axis=0).astype(BF16)

        s = jnp.dot(kaug_ref[pl.ds(q0, blk), :], qaug, preferred_element_type=F32) + bias_ref[2 * blk:3 * blk, :]
        m = jnp.max(s, axis=0, keepdims=True)
        p = jnp.exp(s - m)
        acc = jnp.dot(vt_ref[qb], p.astype(BF16), preferred_element_type=F32)

        def update(st, s, c, vt):
            m, acc = st
            m_new = jnp.maximum(m, jnp.max(s, axis=0, keepdims=True) + c)
            p = jnp.exp(s - (m_new - c))
            return m_new, jnp.exp(m - m_new) * acc + jnp.dot(vt, p.astype(BF16), preferred_element_type=F32)

        def one_block(st):
            c = -slope * (qb * blk).astype(F32)
            s = jnp.dot(kaug_ref[0:blk, :], qaug, preferred_element_type=F32) + bias_ref[blk:2 * blk, :]
            return update(st, s, c, vt_ref[0])

        def two_blocks(i, st):
            n = odd + 2 * i
            k0 = pl.multiple_of(n * blk, blk)
            c = -slope * ((qb - n - 1) * blk).astype(F32)
            s = jnp.dot(kaug_ref[pl.ds(k0, 2 * blk), :], qaug, preferred_element_type=F32) + bias_ref[0:2 * blk, :]
            return update(st, s, c, jnp.concatenate([vt_ref[n], vt_ref[n + 1]], axis=1))

        odd = qb % 2
        m, acc = lax.cond(odd == 1, one_block, lambda st: st, (m, acc))
        m, acc = lax.fori_loop(0, qb // 2, two_blocks, (m, acc))
        o_t = acc[0:hd, :] / acc[hd:hd + 1, :]
        o_ref[0, pl.ds(q0, blk), :] = o_t.T.astype(o_ref.dtype)
        return carry

    lax.fori_loop(0, nblk, q_block, 0)


def _moba_prompt(q, k, v, slopes):
    b, seq, hdm = q.shape
    hd = hdm // B_HEADS
    assert seq % B_BLOCK == 0 and hd == LANES
    nblk = seq // B_BLOCK
    nbp = -(-nblk // SUBLANES) * SUBLANES
    assert nbp <= hd and nblk >= 2
    kern = functools.partial(_moba_prompt_kernel, nblk=nblk, hd=hd)
    spec = pl.BlockSpec((1, seq, hd), lambda i, h: (i, 0, h))
    return pl.pallas_call(
        kern,
        grid=(b, B_HEADS),
        in_specs=[pl.BlockSpec(memory_space=pltpu.SMEM), spec, spec, spec],
        out_specs=spec,
        out_shape=jax.ShapeDtypeStruct((b, seq, hdm), BF16),
        scratch_shapes=[pltpu.VMEM((seq, 2 * hd), BF16),
                        pltpu.VMEM((nblk, hd + V_PAD_ROWS, B_BLOCK), BF16),
                        pltpu.VMEM((nbp, hd), F32),
                        pltpu.VMEM((3 * B_BLOCK, B_BLOCK), F32)],
        compiler_params=_cparams("parallel", "parallel"),
        name="moba_prompt",
    )(slopes, q, k, v)


MEAN_PAGES_PER_STEP = 8


def _page_mean_kernel(pt_ref, *refs, ppb, pps):
    o_ref = refs[pps]
    for blk in range(pps // ppb):
        s = jnp.sum(refs[blk * ppb][0], axis=0, keepdims=True)
        for pg in range(1, ppb):
            s = s + jnp.sum(refs[blk * ppb + pg][0], axis=0, keepdims=True)
        o_ref[0, blk] = s * (1.0 / B_BLOCK)


def _cache_block_means(cache, page_table):
    npool, page, hdm = cache.shape
    db, npg = page_table.shape
    ppb = B_BLOCK // page
    nblk = npg // ppb
    pps = MEAN_PAGES_PER_STEP
    while npg % pps:
        pps //= 2
    assert pps % ppb == 0
    kern = functools.partial(_page_mean_kernel, ppb=ppb, pps=pps)

    def page_spec(r):
        return pl.BlockSpec((1, page, hdm), lambda i, p, pt: (pt[i, p * pps + r], 0, 0))

    out = pl.pallas_call(
        kern,
        grid_spec=pltpu.PrefetchScalarGridSpec(
            num_scalar_prefetch=1,
            grid=(db, npg // pps),
            in_specs=[page_spec(r) for r in range(pps)],
            out_specs=pl.BlockSpec((1, pps // ppb, 1, hdm), lambda i, p, pt: (i, p, 0, 0)),
        ),
        out_shape=jax.ShapeDtypeStruct((db, nblk, 1, hdm), F32),
        compiler_params=_cparams("parallel", "parallel"),
        name="moba_cache_means",
    )(page_table, *([cache] * pps))
    return out.reshape(db, nblk, hdm)


def _moba_select_kernel(q_ref, mean_ref, o_ref, *, s_len, hd, n_valid):
    rows, hdm = q_ref.shape[1], q_ref.shape[2]
    nblk = mean_ref.shape[1]
    r_i = lax.broadcasted_iota(jnp.int32, (rows, hdm), 0)
    c_i = lax.broadcasted_iota(jnp.int32, (rows, hdm), 1)
    qx = jnp.where((c_i // hd) == (r_i // s_len), q_ref[0], 0.0)
    sc = _dot_nt_f32(qx, mean_ref[0])
    lane = lax.broadcasted_iota(jnp.int32, (rows, nblk), 1).astype(F32)
    out_lane = lax.broadcasted_iota(jnp.int32, (rows, LANES), 1)
    out = jnp.zeros((rows, LANES), F32)
    for t in range(n_valid):
        mx = jnp.max(sc, axis=-1, keepdims=True)
        idx = jnp.min(jnp.where(sc == mx, lane, float(nblk)), axis=-1, keepdims=True)
        out = jnp.where(out_lane == t, idx, out)
        sc = jnp.where(lane == idx, NEG_INF, sc)
    o_ref[0] = out.astype(jnp.int32)


def _moba_select(q_rep, means, s_len, n_valid):
    db, rows, hdm = q_rep.shape
    nblk = means.shape[1]
    kern = functools.partial(_moba_select_kernel, s_len=s_len, hd=hdm // B_HEADS, n_valid=n_valid)
    return pl.pallas_call(
        kern,
        grid=(db,),
        in_specs=[pl.BlockSpec((1, rows, hdm), lambda i: (i, 0, 0)),
                  pl.BlockSpec((1, nblk, hdm), lambda i: (i, 0, 0))],
        out_specs=pl.BlockSpec((1, rows, LANES), lambda i: (i, 0, 0)),
        out_shape=jax.ShapeDtypeStruct((db, rows, LANES), jnp.int32),
        compiler_params=_cparams("parallel"),
        name="moba_sample_select",
    )(q_rep, means)


def _moba_sample_kernel(phys_ref, idx_ref, slope_ref, q_ref, kn_ref, vn_ref, *refs,
                        s_len, hd, past, n_valid, nsel, ppb):
    npages = nsel * ppb
    k_refs, v_refs = refs[:npages], refs[npages:2 * npages]
    o_ref, q8_ref = refs[2 * npages], refs[2 * npages + 1]
    b, h = pl.program_id(0), pl.program_id(1)
    slope = slope_ref[h]
    scale = hd ** -0.5
    page = k_refs[0].shape[1]
    row8 = lax.broadcasted_iota(jnp.int32, (SUBLANES, 1), 0)

    q8_ref[...] = jnp.zeros_like(q8_ref)
    q8_ref[0:s_len, :] = q_ref[0] * scale
    q8 = q8_ref[...]
    kn = kn_ref[0]
    vn = vn_ref[0]
    s_cols = []
    for c in range(s_len):
        sc = jnp.sum(q8 * kn[c:c + 1, :], axis=-1, keepdims=True) - slope * (row8 - c).astype(F32)
        s_cols.append(jnp.where(row8 >= c, sc, NEG_INF))
    m = s_cols[0]
    for c in range(1, s_len):
        m = jnp.maximum(m, s_cols[c])

    qs = q8.astype(BF16)
    col = lax.broadcasted_iota(jnp.int32, (SUBLANES, ppb * page), 1)
    row = lax.broadcasted_iota(jnp.int32, (SUBLANES, ppb * page), 0)
    live = [j for j in range(nsel) if j % B_TOPK < n_valid]
    tiles = []
    for j in live:
        blk_id = idx_ref[(b * B_HEADS + h) * nsel + j]
        kb = jnp.concatenate([k_refs[j * ppb + pg][0] for pg in range(ppb)], axis=0).astype(BF16)
        dist = (past + row - blk_id * B_BLOCK - col).astype(F32)
        s = jnp.where(row == j // B_TOPK, _dot_nt(qs, kb) - slope * dist, NEG_INF)
        tiles.append(s)
        m = jnp.maximum(m, jnp.max(s, axis=-1, keepdims=True))

    l = jnp.zeros((SUBLANES, 1), F32)
    acc = jnp.zeros((SUBLANES, hd), F32)
    for c in range(s_len):
        p = jnp.exp(s_cols[c] - m)
        l = l + p
        acc = acc + p * vn[c:c + 1, :]
    for j, s in zip(live, tiles):
        p = jnp.exp(s - m)
        vb = jnp.concatenate([v_refs[j * ppb + pg][0] for pg in range(ppb)], axis=0).astype(BF16)
        l = l + jnp.sum(p, axis=-1, keepdims=True)
        acc = acc + jnp.dot(p.astype(BF16), vb, preferred_element_type=F32)
    o_ref[0] = (acc / l)[0:s_len, :]


def _moba_sample(q, k_new, v_new, cache_k, cache_v, page_table, slopes):
    db, s_len, hdm = q.shape
    hd = hdm // B_HEADS
    npool, page, _ = cache_k.shape
    npg = page_table.shape[1]
    past = npg * page
    assert B_BLOCK % page == 0 and past % B_BLOCK == 0 and s_len <= SUBLANES
    own_blk = past // B_BLOCK
    assert own_blk >= 1
    n_valid = min(own_blk, B_TOPK)
    nsel = s_len * B_TOPK

    means = _cache_block_means(cache_k, page_table)
    q_rep = jnp.tile(q, (1, B_HEADS, 1))
    top = _moba_select(q_rep, means, s_len, n_valid)[:, :, :B_TOPK]
    top = jnp.minimum(top, own_blk - 1)
    idx = top.reshape(db, B_HEADS, nsel)
    ppb = B_BLOCK // page
    pages = idx[..., None] * ppb + jnp.arange(ppb, dtype=jnp.int32)
    phys = jnp.take_along_axis(page_table[:, None, :], pages.reshape(db, 1, -1), axis=2)
    phys = phys.reshape(-1).astype(jnp.int32)
    idx_flat = idx.reshape(-1).astype(jnp.int32)

    def page_spec(r):
        return pl.BlockSpec((1, page, hd), lambda i, h, ph, ix: (ph[(i * B_HEADS + h) * npages + r], 0, h))

    npages = nsel * ppb
    row_spec = pl.BlockSpec((1, s_len, hd), lambda i, h, ph, ix: (i, 0, h))
    kern = functools.partial(_moba_sample_kernel, s_len=s_len, hd=hd, past=past, n_valid=n_valid, nsel=nsel,
                             ppb=ppb)
    return pl.pallas_call(
        kern,
        grid_spec=pltpu.PrefetchScalarGridSpec(
            num_scalar_prefetch=2,
            grid=(db, B_HEADS),
            in_specs=[pl.BlockSpec(memory_space=pltpu.SMEM), row_spec, row_spec, row_spec]
                     + [page_spec(r) for r in range(npages)] * 2,
            out_specs=row_spec,
            scratch_shapes=[pltpu.VMEM((SUBLANES, hd), F32)],
        ),
        out_shape=jax.ShapeDtypeStruct((db, s_len, hdm), F32),
        compiler_params=_cparams("parallel", "parallel"),
        name="moba_sample_attend",
    )(phys, idx_flat, slopes, q, k_new, v_new, *([cache_k] * npages), *([cache_v] * npages))


def _hgrn_kernel(q_ref, k_ref, i_ref, f_ref, g_ref, ng_ref, s0_ref, y_ref, s_ref, st_ref, gc_ref, o_ref,
                 *, t_step, cs, hps):
    c = pl.program_id(2)
    nsub = cs // SUB
    dk = dv = LANES

    @pl.when(c == 0)
    def _():
        for hh in range(hps):
            st_ref[hh] = s0_ref[0, hh].T

    gc_ref[...] = _tri_cumsum(f_ref[0], cs)
    rsub = lax.broadcasted_iota(jnp.int32, (SUB, 1), 0)

    def chunk(ci, carry):
        r0 = pl.multiple_of(ci * cs, cs)
        for hh in range(hps):
            q = q_ref[0, pl.ds(r0, cs), hh * dk:(hh + 1) * dk]
            k = k_ref[0, pl.ds(r0, cs), hh * dk:(hh + 1) * dk]
            iv = i_ref[0, pl.ds(r0, cs), hh * dv:(hh + 1) * dv]
            gcum = gc_ref[pl.ds(r0, cs), hh * dk:(hh + 1) * dk]
            st = st_ref[hh]
            st16 = st.astype(BF16)
            iv16 = iv.astype(BF16)
            o_inter = _dot_nt((q * jnp.exp(gcum)).astype(BF16), st16)
            for si in range(nsub):
                lo, hi = si * SUB, (si + 1) * SUB
                g_i = gcum[lo:hi]
                q_i = q[lo:hi]
                k_i = k[lo:hi]
                i_i = iv[lo:hi]
                acc = o_inter[lo:hi]
                if si > 0:
                    gref = gcum[lo - 1:lo]
                    qt = (q_i * jnp.exp(g_i - gref)).astype(BF16)
                    kt = (k[:lo] * jnp.exp(gref - gcum[:lo])).astype(BF16)
                    att = _dot_nt(qt, kt)
                    acc = acc + jnp.dot(att.astype(BF16), iv16[:lo], preferred_element_type=F32)
                for s in range(SUB):
                    e = jnp.exp(jnp.where(rsub >= s, g_i - g_i[s:s + 1], NEG_INF))
                    a = jnp.sum(q_i * k_i[s:s + 1] * e, axis=-1, keepdims=True)
                    acc = acc + a * i_i[s:s + 1]
                o_ref[pl.ds(r0 + lo, SUB), hh * dv:(hh + 1) * dv] = acc
            glast = gcum[cs - 1:cs]
            kh = (k * jnp.exp(glast - gcum)).astype(BF16)
            st_ref[hh] = st * jnp.exp(glast) + _dot_tn(iv16, kh)
        return carry

    lax.fori_loop(0, t_step // cs, chunk, 0)
    for hh in range(hps):
        o = o_ref[:, hh * dv:(hh + 1) * dv]
        y = o * lax.rsqrt(jnp.mean(o * o, axis=-1, keepdims=True) + EPS) * ng_ref[...]
        y_ref[0, :, hh * dv:(hh + 1) * dv] = (y * g_ref[0, :, hh * dv:(hh + 1) * dv]).astype(y_ref.dtype)

    @pl.when(c == pl.num_programs(2) - 1)
    def _():
        for hh in range(hps):
            s_ref[0, hh] = st_ref[hh].T


HGRN_HEADS_PER_STEP = 4


def _hgrn_recurrence(q, k, iv, logf, gate, norm_g, s0, t_step, cs):
    b, seq, hdm = q.shape
    dk = C_DK
    dv = hdm // C_HEADS
    hps = HGRN_HEADS_PER_STEP
    assert dk == LANES and dv == LANES and seq % t_step == 0 and t_step % cs == 0 and cs % SUB == 0
    assert C_HEADS % hps == 0
    kern = functools.partial(_hgrn_kernel, t_step=t_step, cs=cs, hps=hps)
    tok = pl.BlockSpec((1, t_step, hps * dv), lambda i, h, c: (i, c, h))
    st = pl.BlockSpec((1, hps, dk, dv), lambda i, h, c: (i, h, 0, 0))
    return pl.pallas_call(
        kern,
        grid=(b, C_HEADS // hps, seq // t_step),
        in_specs=[tok, tok, tok, tok, tok, pl.BlockSpec((1, dv), lambda i, h, c: (0, 0)), st],
        out_specs=[tok, st],
        out_shape=[jax.ShapeDtypeStruct((b, seq, hdm), BF16),
                   jax.ShapeDtypeStruct((b, C_HEADS, dk, dv), F32)],
        scratch_shapes=[pltpu.VMEM((hps, dv, dk), F32), pltpu.VMEM((t_step, hps * dk), F32),
                        pltpu.VMEM((t_step, hps * dv), F32)],
        compiler_params=_cparams("parallel", "parallel", "arbitrary"),
        name="hgrn_recurrence",
    )(q, k, iv, logf, gate, norm_g.reshape(1, dv), s0)


def _mlstm_kernel(q_ref, k_ref, v_ref, og_ref, gt_ref, ng_ref, c0_ref, n0_ref, m0_ref,
                  y_ref, c_ref, n_ref, m_ref, *, c, dk, dv, valid_len):
    @pl.when(pl.program_id(1) == 0)
    def _():
        c_ref[...] = c0_ref[...]
        n_ref[...] = n0_ref[...]
        m_ref[...] = m0_ref[...]

    gt = gt_ref[0]
    rowc = lax.broadcasted_iota(jnp.int32, (c, LANES), 0)
    lanec = lax.broadcasted_iota(jnp.int32, (c, LANES), 1)
    gt = jnp.where(rowc < valid_len, gt, jnp.where(lanec < D_HEADS, -1e30, 0.0))
    cum = _tri_cumsum(gt, c)
    gt_t = gt.T
    cum_t = cum.T
    r_i = lax.broadcasted_iota(jnp.int32, (c, c), 0)
    c_i = lax.broadcasted_iota(jnp.int32, (c, c), 1)
    tril = c_i <= r_i
    for h in range(D_HEADS):
        b_col = cum[:, D_HEADS + h:D_HEADS + h + 1]
        b_row = cum_t[D_HEADS + h:D_HEADS + h + 1, :]
        li_col = gt[:, h:h + 1]
        li_row = gt_t[h:h + 1, :]
        m_prev = m_ref[0, h][:, 0:1]
        n_row = n_ref[0, h]
        c_st = c_ref[0, h]
        qh = q_ref[0, :, h * dk:(h + 1) * dk]
        kh = k_ref[0, :, h * dk:(h + 1) * dk]
        vh16 = v_ref[0, :, h * dv:(h + 1) * dv].astype(BF16)
        qh16 = qh.astype(BF16)

        dmat = jnp.where(tril, b_col - b_row + li_row, NEG_INF)
        inter = b_col + m_prev
        m_t = jnp.maximum(inter, jnp.max(dmat, axis=-1, keepdims=True))
        w_inter = jnp.exp(inter - m_t)
        wqk = jnp.exp(dmat - m_t) * _dot_nt(qh16, kh.astype(BF16))
        num = w_inter * jnp.dot(qh16, c_st.astype(BF16), preferred_element_type=F32) \
            + jnp.dot(wqk.astype(BF16), vh16, preferred_element_type=F32)
        den = w_inter * jnp.sum(qh * n_row, axis=-1, keepdims=True) + jnp.sum(wqk, axis=-1, keepdims=True)
        hh = num / jnp.maximum(jnp.abs(den), jnp.exp(-m_t))

        b_last = b_col[c - 1:c, :]
        a_col = b_last - b_col + li_col
        m_new = jnp.maximum(b_last + m_prev, jnp.max(a_col, axis=0, keepdims=True))
        w_c = jnp.exp(b_last + m_prev - m_new)
        w_s = jnp.exp(a_col - m_new)
        ks = w_s * kh
        c_ref[0, h] = w_c * c_st + _dot_tn(ks.astype(BF16), vh16)
        n_ref[0, h] = w_c * n_row + jnp.sum(ks, axis=0, keepdims=True)
        m_ref[0, h] = jnp.broadcast_to(m_new, (1, LANES))

        y = hh * lax.rsqrt(jnp.mean(hh * hh, axis=-1, keepdims=True) + EPS) * ng_ref[:, h * dv:(h + 1) * dv]
        y_ref[0, :, h * dv:(h + 1) * dv] = (y * og_ref[0, :, h * dv:(h + 1) * dv]).astype(y_ref.dtype)


def _mlstm_recurrence(q, k, v, og, gates, norm_g, c0, n0, m0, c, valid_len):
    b, seq, _ = q.shape
    dk = q.shape[2] // D_HEADS
    dv = v.shape[2] // D_HEADS
    assert seq % c == 0 and (seq == c or valid_len == c)
    n0 = n0.reshape(b, D_HEADS, 1, dk)
    m0 = jnp.broadcast_to(m0.reshape(b, D_HEADS, 1, 1), (b, D_HEADS, 1, LANES))
    kern = functools.partial(_mlstm_kernel, c=c, dk=dk, dv=dv, valid_len=valid_len)

    def tok(w):
        return pl.BlockSpec((1, c, w), lambda i, j: (i, j, 0))

    c_spec = pl.BlockSpec((1, D_HEADS, dk, dv), lambda i, j: (i, 0, 0, 0))
    n_spec = pl.BlockSpec((1, D_HEADS, 1, dk), lambda i, j: (i, 0, 0, 0))
    m_spec = pl.BlockSpec((1, D_HEADS, 1, LANES), lambda i, j: (i, 0, 0, 0))
    y, c_out, n_out, m_out = pl.pallas_call(
        kern,
        grid=(b, seq // c),
        in_specs=[tok(D_HEADS * dk), tok(D_HEADS * dk), tok(D_HEADS * dv), tok(D_HEADS * dv), tok(LANES),
                  pl.BlockSpec((1, D_HEADS * dv), lambda i, j: (0, 0)), c_spec, n_spec, m_spec],
        out_specs=[tok(D_HEADS * dv), c_spec, n_spec, m_spec],
        out_shape=[jax.ShapeDtypeStruct((b, seq, D_HEADS * dv), BF16),
                   jax.ShapeDtypeStruct((b, D_HEADS, dk, dv), F32),
                   jax.ShapeDtypeStruct((b, D_HEADS, 1, dk), F32),
                   jax.ShapeDtypeStruct((b, D_HEADS, 1, LANES), F32)],
        compiler_params=_cparams("parallel", "arbitrary"),
        name="mlstm_recurrence",
    )(q, k, v, og, gates, norm_g.reshape(1, D_HEADS * dv), c0, n0, m0)
    return y, c_out, n_out.reshape(b, D_HEADS, dk), m_out[:, :, 0, 0]


def _ident(z):
    return (z,)


def _gelu(z):
    return (jax.nn.gelu(z, approximate=True),)


def _mix_gmlp(x, g, w_in, ln_g, ln_b, w_s, b_s, w_out):
    b, seq, d = x.shape
    width = w_in.shape[1] // 2
    chunk = w_s.shape[1]
    c = chunk if seq % chunk == 0 else seq
    x2 = x.reshape(b * seq, d)
    (z,) = _norm_matmul(x2, g, w_in, 0, 2 * width, _gelu, (F32,))
    a, v = _gmlp_gate(z.reshape(b * seq // c, c, 2 * width), ln_g, ln_b, w_s, b_s, c)
    y = _matmul_residual(a.reshape(b * seq, width), w_out, x2)
    return y.reshape(b, seq, d), v.reshape(b, seq, width)


def _moba_qkv(x, g, w_qkv):
    b, seq, d = x.shape
    hdm = w_qkv.shape[1] // 3
    x2 = x.reshape(b * seq, d)
    return [_norm_matmul(x2, g, w_qkv, i * hdm, hdm, _ident, (F32,))[0].reshape(b, seq, hdm) for i in range(3)]


def _hgrn_project(x2, g, w_in, lb):
    hdm = w_in.shape[1] // 4
    lb = lb.reshape(1, hdm)

    def silu(z):
        return (z * _sigmoid(z),)

    def forget(z, lbv):
        logf = jnp.log(lbv + (1.0 - lbv) * _sigmoid(z))
        return logf, (1.0 - lbv) * _sigmoid(-z)

    (q,) = _norm_matmul(x2, g, w_in, 0, hdm, silu, (F32,))
    logf, k = _norm_matmul(x2, g, w_in, hdm, hdm, forget, (F32, F32), extras=(lb,))
    (iv,) = _norm_matmul(x2, g, w_in, 2 * hdm, hdm, _ident, (F32,))
    (gate,) = _norm_matmul(x2, g, w_in, 3 * hdm, hdm, silu, (F32,))
    return q, k, iv, logf, gate


def _mix_hgrn(x, g, w_in, lb, norm_g, w_out, s0):
    b, seq, d = x.shape
    x2 = x.reshape(b * seq, d)
    parts = [p.reshape(b, seq, -1) for p in _hgrn_project(x2, g, w_in, lb)]
    if seq % 256 == 0:
        t_step, cs, pad = 256, 64, 0
    else:
        t_step = cs = -(-seq // SUB) * SUB
        pad = t_step - seq
        parts = [jnp.pad(p, ((0, 0), (0, pad), (0, 0))) for p in parts]
    y, s = _hgrn_recurrence(*parts, norm_g, s0, t_step, cs)
    y = y[:, :seq].reshape(b * seq, -1)
    return _matmul_residual(y, w_out, x2).reshape(b, seq, d), s


def _mix_mlstm(x, g, w_in, w_gates, b_gates, norm_g, w_out, c0, n0, m0):
    b, seq, d = x.shape
    dk = c0.shape[2]
    dv = c0.shape[3]
    nq, nv = D_HEADS * dk, D_HEADS * dv
    x2 = x.reshape(b * seq, d)

    def kscale(z):
        return (z * (dk ** -0.5),)

    def ogate(z):
        return (_sigmoid(z),)

    def gates_fn(z, bias):
        zz = z + bias
        lane = lax.broadcasted_iota(jnp.int32, zz.shape, 1)
        return (jnp.where(lane < D_HEADS, zz, _log_sigmoid(zz)),)

    (q,) = _norm_matmul(x2, g, w_in, 0, nq, _ident, (F32,))
    (k,) = _norm_matmul(x2, g, w_in, nq, nq, kscale, (F32,))
    (v,) = _norm_matmul(x2, g, w_in, 2 * nq, nv, _ident, (F32,))
    (og,) = _norm_matmul(x2, g, w_in, 2 * nq + nv, nv, ogate, (F32,))
    (gt,) = _norm_matmul(x2, g, w_gates, 0, LANES, gates_fn, (F32,), extras=(b_gates,))
    parts = [p.reshape(b, seq, -1) for p in (q, k, v, og, gt)]
    if seq % 256 == 0:
        c, valid = 256, 256
    else:
        c = -(-seq // SUB) * SUB
        valid = seq
        parts = [jnp.pad(p, ((0, 0), (0, c - seq), (0, 0))) for p in parts]
    y, c_out, n_out, m_out = _mlstm_recurrence(*parts, norm_g, c0, n0, m0, c, valid)
    y = y[:, :seq].reshape(b * seq, nv)
    return _matmul_residual(y, w_out, x2).reshape(b, seq, d), c_out, n_out, m_out


def kernel(x_prompt, x_sample, cache_k, cache_v, page_table, state_hgrn, state_mlstm_c, state_mlstm_n, state_mlstm_m, norm_mix, norm_ffn, norm_final, w_ffn_up, w_ffn_down, a_w_in, a_ln_g, a_ln_b, a_w_s, a_b_s, a_w_out, b_w_qkv, b_w_out, c_w_in, c_lower_bound, c_norm_g, c_w_out, d_w_in, d_b_gates, d_norm_g, d_w_out):
    depth = norm_mix.shape[0]
    bsz, seq, d = x_prompt.shape
    dbsz, dseq, _ = x_sample.shape
    slopes = jnp.asarray(2.0 ** (-8.0 * np.arange(1, B_HEADS + 1) / B_HEADS), F32)
    lbs = jax.nn.softmax(c_lower_bound.astype(F32), axis=0)
    lbs = jnp.cumsum(lbs, axis=0) - lbs[0]
    hdm = b_w_qkv.shape[2] // 3

    xp, xs = x_prompt, x_sample
    outs = {k: [] for k in ("av", "kp", "vp", "ks", "vs", "hp", "hs", "cp", "np", "mp", "cs", "ns", "ms")}
    for layer in range(depth):
        kind = layer % N_MIXERS
        j = layer // N_MIXERS
        g = norm_mix[layer]
        if kind == 0:
            args = (a_w_in[j].astype(BF16), a_ln_g[j], a_ln_b[j], a_w_s[j], a_b_s[j], a_w_out[j].astype(BF16))
            xp, _ = _mix_gmlp(xp, g, *args)
            xs, vrow = _mix_gmlp(xs, g, *args)
            outs["av"].append(vrow)
        elif kind == 1:
            w_qkv = b_w_qkv[j].astype(BF16)
            w_out = b_w_out[j].astype(BF16)
            qp, kp, vp = _moba_qkv(xp, g, w_qkv)
            op = _moba_prompt(qp, kp, vp, slopes)
            xp = _matmul_residual(op.reshape(bsz * seq, hdm), w_out, xp.reshape(bsz * seq, d)).reshape(bsz, seq, d)
            qs, ks, vs = _moba_qkv(xs, g, w_qkv)
            ck = cache_k.reshape(-1, cache_k.shape[2], hdm)
            cv = cache_v.reshape(-1, cache_v.shape[2], hdm)
            os_ = _moba_sample(qs, ks, vs, ck, cv, page_table + j * cache_k.shape[1], slopes)
            xs = _matmul_residual(os_.reshape(dbsz * dseq, hdm), w_out,
                                  xs.reshape(dbsz * dseq, d)).reshape(dbsz, dseq, d)
            hd = hdm // B_HEADS
            outs["kp"].append(kp.reshape(bsz, seq, B_HEADS, hd))
            outs["vp"].append(vp.reshape(bsz, seq, B_HEADS, hd))
            outs["ks"].append(ks.reshape(dbsz, dseq, B_HEADS, hd))
            outs["vs"].append(vs.reshape(dbsz, dseq, B_HEADS, hd))
        elif kind == 2:
            args = (c_w_in[j].astype(BF16), lbs[layer], c_norm_g[j], c_w_out[j].astype(BF16))
            s0 = jnp.zeros((bsz,) + state_hgrn.shape[2:], F32)
            xp, sp = _mix_hgrn(xp, g, *args, s0)
            xs, ss = _mix_hgrn(xs, g, *args, state_hgrn[j])
            outs["hp"].append(sp)
            outs["hs"].append(ss)
        else:
            ng = 2 * D_HEADS * (state_mlstm_c.shape[3] + state_mlstm_c.shape[4])
            w_in = d_w_in[j]
            w_gates = jnp.pad(w_in[:, ng:], ((0, 0), (0, LANES - 2 * D_HEADS))).astype(BF16)
            b_gates = jnp.pad(d_b_gates[j], (0, LANES - 2 * D_HEADS)).reshape(1, LANES)
            args = (w_in.astype(BF16), w_gates, b_gates, d_norm_g[j], d_w_out[j].astype(BF16))
            zc = jnp.zeros((bsz,) + state_mlstm_c.shape[2:], F32)
            zn = jnp.zeros((bsz,) + state_mlstm_n.shape[2:], F32)
            zm = jnp.zeros((bsz,) + state_mlstm_m.shape[2:], F32)
            xp, cp, np_, mp = _mix_mlstm(xp, g, *args, zc, zn, zm)
            xs, cs, ns, ms = _mix_mlstm(xs, g, *args, state_mlstm_c[j], state_mlstm_n[j], state_mlstm_m[j])
            for key, val in zip(("cp", "np", "mp", "cs", "ns", "ms"), (cp, np_, mp, cs, ns, ms)):
                outs[key].append(val)
        w_up = w_ffn_up[layer].astype(BF16)
        w_down = w_ffn_down[layer].astype(BF16)
        xp = _ffn(xp.reshape(bsz * seq, d), norm_ffn[layer], w_up, w_down).reshape(bsz, seq, d)
        xs = _ffn(xs.reshape(dbsz * dseq, d), norm_ffn[layer], w_up, w_down).reshape(dbsz, dseq, d)
    y_prompt = _rmsnorm(xp.reshape(bsz * seq, d), norm_final).reshape(bsz, seq, d)
    y_sample = _rmsnorm(xs.reshape(dbsz * dseq, d), norm_final).reshape(dbsz, dseq, d)
    st = {k: jnp.stack(v) for k, v in outs.items()}
    return (y_prompt, y_sample, st["av"], st["kp"], st["vp"], st["ks"], st["vs"], st["hp"], st["hs"],
            st["cp"], st["np"], st["mp"], st["cs"], st["ns"], st["ms"])
```

```python
import functools

import jax
import jax.numpy as jnp
import numpy as np
from jax import lax
from jax.experimental import pallas as pl
from jax.experimental.pallas import tpu as pltpu

F32 = jnp.float32
BF16 = jnp.bfloat16
EPS = 1e-6
NEG_INF = float("-inf")

A_GROUPS = 8
B_HEADS = 16
B_BLOCK = 256
B_TOPK = 3
C_HEADS = 16
C_DK = 128
D_HEADS = 4
N_MIXERS = 4

LANES = 128
SUBLANES = 8
VMEM_LIMIT_BYTES = 56 * 1024 * 1024
SUB = 16
MM_TILE_M = 1024
MM_TILE_N = 512
FFN_TILE_F = 512
RES_TILE_N = 1024


def _cparams(*sem):
    return pltpu.CompilerParams(dimension_semantics=sem, vmem_limit_bytes=VMEM_LIMIT_BYTES)


def _sigmoid(x):
    return 1.0 / (1.0 + jnp.exp(-x))


def _log_sigmoid(x):
    return jnp.minimum(x, 0.0) - jnp.log(1.0 + jnp.exp(-jnp.abs(x)))


def _split3(x):
    hi = x.astype(BF16)
    r1 = x - hi.astype(F32)
    mid = r1.astype(BF16)
    lo = (r1 - mid.astype(F32)).astype(BF16)
    return hi, mid, lo


def _tri_cumsum(x, block):
    n = x.shape[0]
    r = lax.broadcasted_iota(jnp.int32, (n, n), 0)
    c = lax.broadcasted_iota(jnp.int32, (n, n), 1)
    tri = jnp.where((c <= r) & ((r // block) == (c // block)), 1.0, 0.0).astype(BF16)
    hi, mid, lo = _split3(x)
    out = jnp.dot(tri, lo, preferred_element_type=F32)
    out = out + jnp.dot(tri, mid, preferred_element_type=F32)
    return out + jnp.dot(tri, hi, preferred_element_type=F32)


def _dot_nt(a, b):
    return lax.dot_general(a, b, (((1,), (1,)), ((), ())), preferred_element_type=F32)


def _dot_tn(a, b):
    return lax.dot_general(a, b, (((0,), (0,)), ((), ())), preferred_element_type=F32)


def _dot_nt_f32(a, b):
    a0, a1, a2 = _split3(a)
    b0, b1, b2 = _split3(b)
    out = _dot_nt(a1, b1) + _dot_nt(a0, b2) + _dot_nt(a2, b0)
    out = out + _dot_nt(a0, b1) + _dot_nt(a1, b0)
    return out + _dot_nt(a0, b0)


def _norm_mm_kernel(x_ref, g_ref, w_ref, *rest, epilogue, n_extra, n_out):
    extra = rest[:n_extra]
    outs = rest[n_extra:n_extra + n_out]
    xn_ref = rest[n_extra + n_out]

    @pl.when(pl.program_id(1) == 0)
    def _():
        x = x_ref[...]
        y = x * lax.rsqrt(jnp.mean(x * x, axis=-1, keepdims=True) + EPS)
        xn_ref[...] = (y * g_ref[...]).astype(BF16)

    z = jnp.dot(xn_ref[...], w_ref[...].astype(BF16), preferred_element_type=F32)
    res = epilogue(z, *[e[...] for e in extra])
    for o_ref, r in zip(outs, res):
        o_ref[...] = r.astype(o_ref.dtype)


def _norm_matmul(x, g, w, layer, col0, ncols, epilogue, out_dtypes, extras=()):
    m, d = x.shape
    tm = min(m, MM_TILE_M)
    tn = min(ncols, MM_TILE_N)
    assert m % tm == 0 and ncols % tn == 0 and col0 % tn == 0
    jb = col0 // tn
    kern = functools.partial(_norm_mm_kernel, epilogue=epilogue, n_extra=len(extras), n_out=len(out_dtypes))
    return pl.pallas_call(
        kern,
        grid=(m // tm, ncols // tn),
        in_specs=[pl.BlockSpec((tm, d), lambda i, j: (i, 0)),
                  pl.BlockSpec((1, d), lambda i, j: (0, 0)),
                  pl.BlockSpec((None, d, tn), lambda i, j: (layer, 0, j + jb))]
                 + [pl.BlockSpec((1, tn), lambda i, j: (0, j)) for _ in extras],
        out_specs=[pl.BlockSpec((tm, tn), lambda i, j: (i, j)) for _ in out_dtypes],
        out_shape=[jax.ShapeDtypeStruct((m, ncols), dt) for dt in out_dtypes],
        scratch_shapes=[pltpu.VMEM((tm, d), BF16)],
        compiler_params=_cparams("parallel", "arbitrary"),
        name="norm_matmul",
    )(x, g.reshape(1, d), w, *extras)


def _mm_res_kernel(a_ref, w_ref, r_ref, o_ref):
    o_ref[...] = r_ref[...] + jnp.dot(a_ref[...].astype(BF16), w_ref[...].astype(BF16),
                                      preferred_element_type=F32)


def _matmul_residual(a, w, layer, res):
    m, k = a.shape
    n = w.shape[2]
    tm = min(m, MM_TILE_M)
    tn = min(n, RES_TILE_N)
    assert m % tm == 0 and n % tn == 0
    return pl.pallas_call(
        _mm_res_kernel,
        grid=(m // tm, n // tn),
        in_specs=[pl.BlockSpec((tm, k), lambda i, j: (i, 0)),
                  pl.BlockSpec((None, k, tn), lambda i, j: (layer, 0, j)),
                  pl.BlockSpec((tm, tn), lambda i, j: (i, j))],
        out_specs=pl.BlockSpec((tm, tn), lambda i, j: (i, j)),
        out_shape=jax.ShapeDtypeStruct((m, n), F32),
        compiler_params=_cparams("parallel", "parallel"),
        name="matmul_residual",
    )(a, w, res)


def _ffn_kernel(x_ref, g_ref, wu_ref, wd_ref, o_ref, xn_ref):
    f = pl.program_id(1)

    @pl.when(f == 0)
    def _():
        x = x_ref[...]
        y = x * lax.rsqrt(jnp.mean(x * x, axis=-1, keepdims=True) + EPS)
        xn_ref[...] = (y * g_ref[...]).astype(BF16)
        o_ref[...] = x

    h = jnp.maximum(jnp.dot(xn_ref[...], wu_ref[...].astype(BF16), preferred_element_type=F32), 0.0)
    o_ref[...] += jnp.dot((h * h).astype(BF16), wd_ref[...].astype(BF16), preferred_element_type=F32)


def _ffn(x, g, w_up, w_down, layer):
    m, d = x.shape
    dff = w_up.shape[2]
    tm = min(m, MM_TILE_M)
    tf = min(dff, FFN_TILE_F)
    assert m % tm == 0 and dff % tf == 0
    return pl.pallas_call(
        _ffn_kernel,
        grid=(m // tm, dff // tf),
        in_specs=[pl.BlockSpec((tm, d), lambda i, f: (i, 0), pipeline_mode=pl.Buffered(1)),
                  pl.BlockSpec((1, d), lambda i, f: (0, 0)),
                  pl.BlockSpec((None, d, tf), lambda i, f: (layer, 0, f)),
                  pl.BlockSpec((None, tf, d), lambda i, f: (layer, f, 0))],
        out_specs=pl.BlockSpec((tm, d), lambda i, f: (i, 0)),
        out_shape=jax.ShapeDtypeStruct((m, d), F32),
        scratch_shapes=[pltpu.VMEM((tm, d), BF16)],
        compiler_params=_cparams("parallel", "arbitrary"),
        name="ffn",
    )(x, g.reshape(1, d), w_up, w_down)


def _rmsnorm_kernel(x_ref, g_ref, o_ref):
    x = x_ref[...]
    o_ref[...] = x * lax.rsqrt(jnp.mean(x * x, axis=-1, keepdims=True) + EPS) * g_ref[...]


def _rmsnorm(x, g):
    m, d = x.shape
    tm = min(m, 512)
    return pl.pallas_call(
        _rmsnorm_kernel,
        grid=(m // tm,),
        in_specs=[pl.BlockSpec((tm, d), lambda i: (i, 0)), pl.BlockSpec((1, d), lambda i: (0, 0))],
        out_specs=pl.BlockSpec((tm, d), lambda i: (i, 0)),
        out_shape=jax.ShapeDtypeStruct((m, d), F32),
        compiler_params=_cparams("parallel"),
        name="final_rmsnorm",
    )(x, g.reshape(1, d))


def _gmlp_gate_kernel(z_ref, lg_ref, lb_ref, ws_ref, bs_ref, a_ref, v_ref, *, c, width):
    gw = width // A_GROUPS
    z = z_ref[0]
    u = z[:, :width]
    vr = z[:, width:]
    mu = jnp.mean(vr, axis=-1, keepdims=True)
    vc = vr - mu
    v = vc * lax.rsqrt(jnp.mean(vc * vc, axis=-1, keepdims=True) + EPS) * lg_ref[...] + lb_ref[...]
    v_ref[0] = v
    row = lax.broadcasted_iota(jnp.int32, (c, c), 0)
    col = lax.broadcasted_iota(jnp.int32, (c, c), 1)
    bs = bs_ref[...]
    for g in range(A_GROUPS):
        ws = jnp.where(col <= row, ws_ref[g], 0.0)
        vg = v[:, g * gw:(g + 1) * gw]
        if c >= 2 * SUBLANES:
            s = jnp.dot(ws.astype(BF16), vg.astype(BF16), preferred_element_type=F32)
        else:
            s = jnp.zeros((c, gw), F32)
            for t in range(c):
                s = s + ws[:, t:t + 1] * vg[t:t + 1, :]
        s = s + bs[:, g:g + 1]
        a_ref[0, :, g * gw:(g + 1) * gw] = (u[:, g * gw:(g + 1) * gw] * s).astype(a_ref.dtype)


def _gmlp_gate(z, ln_g, ln_b, w_s, b_s, c):
    nb, _, w2 = z.shape
    width = w2 // 2
    ws = w_s[:, :c, :c]
    bs_t = b_s[:, :c].T
    kern = functools.partial(_gmlp_gate_kernel, c=c, width=width)
    return pl.pallas_call(
        kern,
        grid=(nb,),
        in_specs=[pl.BlockSpec((1, c, w2), lambda i: (i, 0, 0)),
                  pl.BlockSpec((1, width), lambda i: (0, 0)),
                  pl.BlockSpec((1, width), lambda i: (0, 0)),
                  pl.BlockSpec((A_GROUPS, c, c), lambda i: (0, 0, 0)),
                  pl.BlockSpec((c, A_GROUPS), lambda i: (0, 0))],
        out_specs=[pl.BlockSpec((1, c, width), lambda i: (i, 0, 0)),
                   pl.BlockSpec((1, c, width), lambda i: (i, 0, 0))],
        out_shape=[jax.ShapeDtypeStruct((nb, c, width), BF16 if c % (2 * SUBLANES) == 0 else F32),
                   jax.ShapeDtypeStruct((nb, c, width), F32)],
        compiler_params=_cparams("parallel"),
        name="gmlp_gate",
    )(z, ln_g.reshape(1, width), ln_b.reshape(1, width), ws, bs_t)


MASKED = -1e30
V_PAD_ROWS = 16


def _moba_prompt_kernel(slope_ref, q_ref, k_ref, v_ref, o_ref, kaug_ref, vt_ref, means_ref, bias_ref, *, nblk, hd):
    blk = B_BLOCK
    slope = slope_ref[pl.program_id(1)]
    scale = hd ** -0.5
    nbp = means_ref.shape[0]

    lane_blk = lax.broadcasted_iota(jnp.int32, (blk, hd), 1)
    ones_rows = jnp.where(lax.broadcasted_iota(jnp.int32, (V_PAD_ROWS, blk), 0) == 0, 1.0, 0.0).astype(BF16)
    means_ref[...] = jnp.zeros_like(means_ref)
    for n in range(nblk):
        kb = k_ref[0, n * blk:(n + 1) * blk, :]
        means_ref[n:n + 1, :] = jnp.sum(kb, axis=0, keepdims=True) * (1.0 / blk)
        kaug_ref[n * blk:(n + 1) * blk, 0:hd] = kb.astype(BF16)
        kaug_ref[n * blk:(n + 1) * blk, hd:2 * hd] = jnp.where(lane_blk == n, 1.0, 0.0).astype(BF16)
        vt_ref[n, 0:hd, :] = v_ref[0, n * blk:(n + 1) * blk, :].T.astype(BF16)
        vt_ref[n, hd:hd + V_PAD_ROWS, :] = ones_rows
    mean_hi, mean_mid, _ = _split3(means_ref[...])

    key_i = lax.broadcasted_iota(jnp.int32, (blk, blk), 0)
    qry_i = lax.broadcasted_iota(jnp.int32, (blk, blk), 1)
    dist = (qry_i - key_i).astype(F32)
    bias_ref[0:blk, :] = -slope * (dist + blk)
    bias_ref[blk:2 * blk, :] = -slope * dist
    bias_ref[2 * blk:3 * blk, :] = jnp.where(dist >= 0, -slope * dist, NEG_INF)
    rown = lax.broadcasted_iota(jnp.int32, (nbp, blk), 0)

    def q_block(qb, carry):
        q0 = pl.multiple_of(qb * blk, blk)
        q = q_ref[0, pl.ds(q0, blk), :]
        q_hi, q_mid, _ = _split3(q)
        sc = _dot_nt(mean_hi, q_mid) + _dot_nt(mean_mid, q_hi) + _dot_nt(mean_hi, q_hi)
        sc = jnp.where(rown < qb, sc, NEG_INF)
        cnt = jnp.zeros((nbp, blk), F32)
        for n2 in range(nblk):
            rn = sc[n2:n2 + 1, :]
            ahead = (rn > sc) | ((rn == sc) & (rown > n2))
            cnt = cnt + jnp.where(ahead, 1.0, 0.0)
        keep = ((cnt < B_TOPK) & (rown < qb)) | (rown == qb)
        selb = jnp.where(keep, 0.0, MASKED)
        qaug = jnp.concatenate([(q * scale).T, selb, jnp.zeros((hd - nbp, blk), F32)], axis=0).astype(BF16)

        s = jnp.dot(kaug_ref[pl.ds(q0, blk), :], qaug, preferred_element_type=F32) + bias_ref[2 * blk:3 * blk, :]
        m = jnp.max(s, axis=0, keepdims=True)
        p = jnp.exp(s - m)
        acc = jnp.dot(vt_ref[qb], p.astype(BF16), preferred_element_type=F32)

        def update(st, s, c, vt):
            m, acc = st
            m_new = jnp.maximum(m, jnp.max(s, axis=0, keepdims=True) + c)
            p = jnp.exp(s - (m_new - c))
            return m_new, jnp.exp(m - m_new) * acc + jnp.dot(vt, p.astype(BF16), preferred_element_type=F32)

        def one_block(st):
            c = -slope * lax.convert_element_type(qb * blk, F32)
            s = jnp.dot(kaug_ref[0:blk, :], qaug, preferred_element_type=F32) + bias_ref[blk:2 * blk, :]
            return update(st, s, c, vt_ref[0])

        def two_blocks(i, st):
            n = odd + 2 * i
            k0 = pl.multiple_of(n * blk, blk)
            c = -slope * lax.convert_element_type((qb - n - 1) * blk, F32)
            s = jnp.dot(kaug_ref[pl.ds(k0, 2 * blk), :], qaug, preferred_element_type=F32) + bias_ref[0:2 * blk, :]
            return update(st, s, c, jnp.concatenate([vt_ref[n], vt_ref[n + 1]], axis=1))

        odd = qb % 2
        m, acc = lax.cond(odd == 1, one_block, lambda st: st, (m, acc))
        m, acc = lax.fori_loop(0, qb // 2, two_blocks, (m, acc))
        o_t = acc[0:hd, :] / acc[hd:hd + 1, :]
        o_ref[0, pl.ds(q0, blk), :] = o_t.T.astype(o_ref.dtype)
        return carry

    lax.fori_loop(0, nblk, q_block, 0)


def _moba_prompt(q, k, v, slopes):
    b, seq, hdm = q.shape
    hd = hdm // B_HEADS
    assert seq % B_BLOCK == 0 and hd == LANES
    nblk = seq // B_BLOCK
    nbp = -(-nblk // SUBLANES) * SUBLANES
    assert nbp <= hd and nblk >= 2
    kern = functools.partial(_moba_prompt_kernel, nblk=nblk, hd=hd)
    spec = pl.BlockSpec((1, seq, hd), lambda i, h: (i, 0, h))
    return pl.pallas_call(
        kern,
        grid=(b, B_HEADS),
        in_specs=[pl.BlockSpec(memory_space=pltpu.SMEM), spec, spec, spec],
        out_specs=spec,
        out_shape=jax.ShapeDtypeStruct((b, seq, hdm), BF16),
        scratch_shapes=[pltpu.VMEM((seq, 2 * hd), BF16),
                        pltpu.VMEM((nblk, hd + V_PAD_ROWS, B_BLOCK), BF16),
                        pltpu.VMEM((nbp, hd), F32),
                        pltpu.VMEM((3 * B_BLOCK, B_BLOCK), F32)],
        compiler_params=_cparams("parallel", "parallel"),
        name="moba_prompt",
    )(slopes, q, k, v)


MEAN_PAGES_PER_STEP = 8


def _page_mean_kernel(pt_ref, *refs, ppb, pps):
    o_ref = refs[pps]
    for blk in range(pps // ppb):
        s = jnp.sum(refs[blk * ppb][0, 0], axis=0)
        for pg in range(1, ppb):
            s = s + jnp.sum(refs[blk * ppb + pg][0, 0], axis=0)
        o_ref[0, blk] = s * (1.0 / B_BLOCK)


def _cache_block_means(cache, layer, page_table):
    _, npool, page, heads, hd = cache.shape
    db, npg = page_table.shape
    ppb = B_BLOCK // page
    nblk = npg // ppb
    pps = MEAN_PAGES_PER_STEP
    while npg % pps:
        pps //= 2
    assert pps % ppb == 0
    kern = functools.partial(_page_mean_kernel, ppb=ppb, pps=pps)

    def page_spec(r):
        return pl.BlockSpec((1, 1, page, heads, hd), lambda i, p, pt: (layer, pt[i, p * pps + r], 0, 0, 0))

    return pl.pallas_call(
        kern,
        grid_spec=pltpu.PrefetchScalarGridSpec(
            num_scalar_prefetch=1,
            grid=(db, npg // pps),
            in_specs=[page_spec(r) for r in range(pps)],
            out_specs=pl.BlockSpec((1, pps // ppb, heads, hd), lambda i, p, pt: (i, p, 0, 0)),
        ),
        out_shape=jax.ShapeDtypeStruct((db, nblk, heads, hd), F32),
        compiler_params=_cparams("parallel", "parallel"),
        name="moba_cache_means",
    )(page_table, *([cache] * pps))


def _moba_select_kernel(q_ref, mean_ref, o_ref, *, s_len, hd, n_valid):
    rows, hdm = q_ref.shape[1], q_ref.shape[2]
    nblk = mean_ref.shape[1]
    r_i = lax.broadcasted_iota(jnp.int32, (rows, hdm), 0)
    c_i = lax.broadcasted_iota(jnp.int32, (rows, hdm), 1)
    qx = jnp.where((c_i // hd) == (r_i // s_len), q_ref[0], 0.0)
    sc = _dot_nt_f32(qx, mean_ref[0])
    lane = lax.broadcasted_iota(jnp.int32, (rows, nblk), 1).astype(F32)
    out_lane = lax.broadcasted_iota(jnp.int32, (rows, LANES), 1)
    out = jnp.zeros((rows, LANES), F32)
    for t in range(n_valid):
        mx = jnp.max(sc, axis=-1, keepdims=True)
        idx = jnp.min(jnp.where(sc == mx, lane, float(nblk)), axis=-1, keepdims=True)
        out = jnp.where(out_lane == t, idx, out)
        sc = jnp.where(lane == idx, NEG_INF, sc)
    o_ref[0] = out.astype(jnp.int32)


def _moba_select(q_rep, means, s_len, n_valid):
    db, rows, hdm = q_rep.shape
    nblk = means.shape[1]
    kern = functools.partial(_moba_select_kernel, s_len=s_len, hd=hdm // B_HEADS, n_valid=n_valid)
    return pl.pallas_call(
        kern,
        grid=(db,),
        in_specs=[pl.BlockSpec((1, rows, hdm), lambda i: (i, 0, 0)),
                  pl.BlockSpec((1, nblk, hdm), lambda i: (i, 0, 0))],
        out_specs=pl.BlockSpec((1, rows, LANES), lambda i: (i, 0, 0)),
        out_shape=jax.ShapeDtypeStruct((db, rows, LANES), jnp.int32),
        compiler_params=_cparams("parallel"),
        name="moba_sample_select",
    )(q_rep, means)


def _moba_sample_kernel(phys_ref, idx_ref, slope_ref, q_ref, kn_ref, vn_ref, ck_hbm, cv_hbm, o_ref,
                        kbuf, vbuf, sems, q8_ref, *, layer, s_len, hd, past, n_valid, nsel, ppb):
    npages = nsel * ppb
    g = pl.program_id(0)
    h = g % B_HEADS
    slot = g % 2
    slope = slope_ref[h]
    scale = hd ** -0.5
    page = kbuf.shape[2]
    row8 = lax.broadcasted_iota(jnp.int32, (SUBLANES, 1), 0)

    def page_copies(step, to_slot):
        head = step % B_HEADS
        out = []
        for r in range(npages):
            pool_row = phys_ref[step * npages + r]
            out.append(pltpu.make_async_copy(ck_hbm.at[layer, pool_row, :, head, :], kbuf.at[to_slot, r],
                                             sems.at[0, to_slot]))
            out.append(pltpu.make_async_copy(cv_hbm.at[layer, pool_row, :, head, :], vbuf.at[to_slot, r],
                                             sems.at[1, to_slot]))
        return out

    @pl.when(g == 0)
    def _():
        for cp in page_copies(g, slot):
            cp.start()

    @pl.when(g + 1 < pl.num_programs(0))
    def _():
        for cp in page_copies(g + 1, 1 - slot):
            cp.start()

    for cp in page_copies(g, slot):
        cp.wait()
    k_refs = [kbuf.at[slot, r] for r in range(npages)]
    v_refs = [vbuf.at[slot, r] for r in range(npages)]

    q8_ref[...] = jnp.zeros_like(q8_ref)
    q8_ref[0:s_len, :] = q_ref[0] * scale
    q8 = q8_ref[...]
    kn = kn_ref[0]
    vn = vn_ref[0]
    s_cols = []
    for c in range(s_len):
        sc = jnp.sum(q8 * kn[c:c + 1, :], axis=-1, keepdims=True) - slope * (row8 - c).astype(F32)
        s_cols.append(jnp.where(row8 >= c, sc, NEG_INF))
    m = s_cols[0]
    for c in range(1, s_len):
        m = jnp.maximum(m, s_cols[c])

    qs = q8.astype(BF16)
    col = lax.broadcasted_iota(jnp.int32, (SUBLANES, ppb * page), 1)
    row = lax.broadcasted_iota(jnp.int32, (SUBLANES, ppb * page), 0)
    live = [j for j in range(nsel) if j % B_TOPK < n_valid]
    tiles = []
    for j in live:
        blk_id = idx_ref[g * nsel + j]
        kb = jnp.concatenate([k_refs[j * ppb + pg][...] for pg in range(ppb)], axis=0).astype(BF16)
        dist = (past + row - blk_id * B_BLOCK - col).astype(F32)
        s = jnp.where(row == j // B_TOPK, _dot_nt(qs, kb) - slope * dist, NEG_INF)
        tiles.append(s)
        m = jnp.maximum(m, jnp.max(s, axis=-1, keepdims=True))

    l = jnp.zeros((SUBLANES, 1), F32)
    acc = jnp.zeros((SUBLANES, hd), F32)
    for c in range(s_len):
        p = jnp.exp(s_cols[c] - m)
        l = l + p
        acc = acc + p * vn[c:c + 1, :]
    for j, s in zip(live, tiles):
        p = jnp.exp(s - m)
        vb = jnp.concatenate([v_refs[j * ppb + pg][...] for pg in range(ppb)], axis=0).astype(BF16)
        l = l + jnp.sum(p, axis=-1, keepdims=True)
        acc = acc + jnp.dot(p.astype(BF16), vb, preferred_element_type=F32)
    o_ref[0] = (acc / l)[0:s_len, :]


def _moba_sample(q, k_new, v_new, cache_k, cache_v, layer, page_table, slopes):
    db, s_len, hdm = q.shape
    hd = hdm // B_HEADS
    page = cache_k.shape[2]
    npg = page_table.shape[1]
    past = npg * page
    assert B_BLOCK % page == 0 and past % B_BLOCK == 0 and s_len <= SUBLANES
    own_blk = past // B_BLOCK
    assert own_blk >= 1
    n_valid = min(own_blk, B_TOPK)
    nsel = s_len * B_TOPK

    means = _cache_block_means(cache_k, layer, page_table).reshape(db, own_blk, hdm)
    q_rep = jnp.tile(q, (1, B_HEADS, 1))
    top = _moba_select(q_rep, means, s_len, n_valid)[:, :, :B_TOPK]
    top = jnp.minimum(top, own_blk - 1)
    idx = top.reshape(db, B_HEADS, nsel)
    ppb = B_BLOCK // page
    pages = idx[..., None] * ppb + jnp.arange(ppb, dtype=jnp.int32)
    phys = jnp.take_along_axis(page_table[:, None, :], pages.reshape(db, 1, -1), axis=2)
    phys = phys.reshape(-1).astype(jnp.int32)
    idx_flat = idx.reshape(-1).astype(jnp.int32)

    npages = nsel * ppb
    row_spec = pl.BlockSpec((1, s_len, hd), lambda g, ph, ix: (g // B_HEADS, 0, g % B_HEADS))
    kern = functools.partial(_moba_sample_kernel, layer=layer, s_len=s_len, hd=hd, past=past, n_valid=n_valid,
                             nsel=nsel, ppb=ppb)
    return pl.pallas_call(
        kern,
        grid_spec=pltpu.PrefetchScalarGridSpec(
            num_scalar_prefetch=2,
            grid=(db * B_HEADS,),
            in_specs=[pl.BlockSpec(memory_space=pltpu.SMEM), row_spec, row_spec, row_spec,
                      pl.BlockSpec(memory_space=pl.ANY), pl.BlockSpec(memory_space=pl.ANY)],
            out_specs=row_spec,
            scratch_shapes=[pltpu.VMEM((2, npages, page, hd), F32), pltpu.VMEM((2, npages, page, hd), F32),
                            pltpu.SemaphoreType.DMA((2, 2)), pltpu.VMEM((SUBLANES, hd), F32)],
        ),
        out_shape=jax.ShapeDtypeStruct((db, s_len, hdm), F32),
        compiler_params=_cparams("arbitrary"),
        name="moba_sample_attend",
    )(phys, idx_flat, slopes, q, k_new, v_new, cache_k, cache_v)


def _hgrn_kernel(q_ref, k_ref, i_ref, f_ref, g_ref, ng_ref, s0_ref, y_ref, s_ref, st_ref, gc_ref, o_ref,
                 *, t_step, cs, hps):
    c = pl.program_id(2)
    nsub = cs // SUB
    dk = dv = LANES

    @pl.when(c == 0)
    def _():
        for hh in range(hps):
            st_ref[hh] = s0_ref[0, hh].T

    gc_ref[...] = _tri_cumsum(f_ref[0], cs)
    rsub = lax.broadcasted_iota(jnp.int32, (SUB, 1), 0)

    def chunk(ci, carry):
        r0 = pl.multiple_of(ci * cs, cs)
        for hh in range(hps):
            q = q_ref[0, pl.ds(r0, cs), hh * dk:(hh + 1) * dk]
            k = k_ref[0, pl.ds(r0, cs), hh * dk:(hh + 1) * dk]
            iv = i_ref[0, pl.ds(r0, cs), hh * dv:(hh + 1) * dv]
            gcum = gc_ref[pl.ds(r0, cs), hh * dk:(hh + 1) * dk]
            st = st_ref[hh]
            st16 = st.astype(BF16)
            iv16 = iv.astype(BF16)
            o_inter = _dot_nt((q * jnp.exp(gcum)).astype(BF16), st16)
            for si in range(nsub):
                lo, hi = si * SUB, (si + 1) * SUB
                g_i = gcum[lo:hi]
                q_i = q[lo:hi]
                k_i = k[lo:hi]
                i_i = iv[lo:hi]
                acc = o_inter[lo:hi]
                if si > 0:
                    gref = gcum[lo - 1:lo]
                    qt = (q_i * jnp.exp(g_i - gref)).astype(BF16)
                    kt = (k[:lo] * jnp.exp(gref - gcum[:lo])).astype(BF16)
                    att = _dot_nt(qt, kt)
                    acc = acc + jnp.dot(att.astype(BF16), iv16[:lo], preferred_element_type=F32)
                for s in range(SUB):
                    e = jnp.exp(jnp.where(rsub >= s, g_i - g_i[s:s + 1], NEG_INF))
                    a = jnp.sum(q_i * k_i[s:s + 1] * e, axis=-1, keepdims=True)
                    acc = acc + a * i_i[s:s + 1]
                o_ref[pl.ds(r0 + lo, SUB), hh * dv:(hh + 1) * dv] = acc
            glast = gcum[cs - 1:cs]
            kh = (k * jnp.exp(glast - gcum)).astype(BF16)
            st_ref[hh] = st * jnp.exp(glast) + _dot_tn(iv16, kh)
        return carry

    lax.fori_loop(0, t_step // cs, chunk, 0)
    for hh in range(hps):
        o = o_ref[:, hh * dv:(hh + 1) * dv]
        y = o * lax.rsqrt(jnp.mean(o * o, axis=-1, keepdims=True) + EPS) * ng_ref[...]
        y_ref[0, :, hh * dv:(hh + 1) * dv] = (y * g_ref[0, :, hh * dv:(hh + 1) * dv]).astype(y_ref.dtype)

    @pl.when(c == pl.num_programs(2) - 1)
    def _():
        for hh in range(hps):
            s_ref[0, hh] = st_ref[hh].T


HGRN_HEADS_PER_STEP = 4


def _hgrn_recurrence(q, k, iv, logf, gate, norm_g, s0, t_step, cs):
    b, seq, hdm = q.shape
    dk = C_DK
    dv = hdm // C_HEADS
    hps = HGRN_HEADS_PER_STEP
    assert dk == LANES and dv == LANES and seq % t_step == 0 and t_step % cs == 0 and cs % SUB == 0
    assert C_HEADS % hps == 0
    kern = functools.partial(_hgrn_kernel, t_step=t_step, cs=cs, hps=hps)
    tok = pl.BlockSpec((1, t_step, hps * dv), lambda i, h, c: (i, c, h))
    st = pl.BlockSpec((1, hps, dk, dv), lambda i, h, c: (i, h, 0, 0))
    return pl.pallas_call(
        kern,
        grid=(b, C_HEADS // hps, seq // t_step),
        in_specs=[tok, tok, tok, tok, tok, pl.BlockSpec((1, dv), lambda i, h, c: (0, 0)), st],
        out_specs=[tok, st],
        out_shape=[jax.ShapeDtypeStruct((b, seq, hdm), BF16),
                   jax.ShapeDtypeStruct((b, C_HEADS, dk, dv), F32)],
        scratch_shapes=[pltpu.VMEM((hps, dv, dk), F32), pltpu.VMEM((t_step, hps * dk), F32),
                        pltpu.VMEM((t_step, hps * dv), F32)],
        compiler_params=_cparams("parallel", "parallel", "arbitrary"),
        name="hgrn_recurrence",
    )(q, k, iv, logf, gate, norm_g.reshape(1, dv), s0)


def _mlstm_kernel(q_ref, k_ref, v_ref, og_ref, gt_ref, ng_ref, c0_ref, n0_ref, m0_ref,
                  y_ref, c_ref, n_ref, m_ref, *, c, dk, dv, valid_len):
    @pl.when(pl.program_id(1) == 0)
    def _():
        c_ref[...] = c0_ref[...]
        n_ref[...] = n0_ref[...]
        m_ref[...] = m0_ref[...]

    gt = gt_ref[0]
    rowc = lax.broadcasted_iota(jnp.int32, (c, LANES), 0)
    lanec = lax.broadcasted_iota(jnp.int32, (c, LANES), 1)
    gt = jnp.where(rowc < valid_len, gt, jnp.where(lanec < D_HEADS, -1e30, 0.0))
    cum = _tri_cumsum(gt, c)
    gt_t = gt.T
    cum_t = cum.T
    r_i = lax.broadcasted_iota(jnp.int32, (c, c), 0)
    c_i = lax.broadcasted_iota(jnp.int32, (c, c), 1)
    tril = c_i <= r_i
    for h in range(D_HEADS):
        b_col = cum[:, D_HEADS + h:D_HEADS + h + 1]
        b_row = cum_t[D_HEADS + h:D_HEADS + h + 1, :]
        li_col = gt[:, h:h + 1]
        li_row = gt_t[h:h + 1, :]
        m_prev = m_ref[0, h][:, 0:1]
        n_row = n_ref[0, h]
        c_st = c_ref[0, h]
        qh = q_ref[0, :, h * dk:(h + 1) * dk]
        kh = k_ref[0, :, h * dk:(h + 1) * dk]
        vh16 = v_ref[0, :, h * dv:(h + 1) * dv].astype(BF16)
        qh16 = qh.astype(BF16)

        dmat = jnp.where(tril, b_col - b_row + li_row, NEG_INF)
        inter = b_col + m_prev
        m_t = jnp.maximum(inter, jnp.max(dmat, axis=-1, keepdims=True))
        w_inter = jnp.exp(inter - m_t)
        wqk = jnp.exp(dmat - m_t) * _dot_nt(qh16, kh.astype(BF16))
        num = w_inter * jnp.dot(qh16, c_st.astype(BF16), preferred_element_type=F32) \
            + jnp.dot(wqk.astype(BF16), vh16, preferred_element_type=F32)
        den = w_inter * jnp.sum(qh * n_row, axis=-1, keepdims=True) + jnp.sum(wqk, axis=-1, keepdims=True)
        hh = num / jnp.maximum(jnp.abs(den), jnp.exp(-m_t))

        b_last = b_col[c - 1:c, :]
        a_col = b_last - b_col + li_col
        m_new = jnp.maximum(b_last + m_prev, jnp.max(a_col, axis=0, keepdims=True))
        w_c = jnp.exp(b_last + m_prev - m_new)
        w_s = jnp.exp(a_col - m_new)
        ks = w_s * kh
        c_ref[0, h] = w_c * c_st + _dot_tn(ks.astype(BF16), vh16)
        n_ref[0, h] = w_c * n_row + jnp.sum(ks, axis=0, keepdims=True)
        m_ref[0, h] = jnp.broadcast_to(m_new, (1, LANES))

        y = hh * lax.rsqrt(jnp.mean(hh * hh, axis=-1, keepdims=True) + EPS) * ng_ref[:, h * dv:(h + 1) * dv]
        y_ref[0, :, h * dv:(h + 1) * dv] = (y * og_ref[0, :, h * dv:(h + 1) * dv]).astype(y_ref.dtype)


def _mlstm_recurrence(q, k, v, og, gates, norm_g, c0, n0, m0, c, valid_len):
    b, seq, _ = q.shape
    dk = q.shape[2] // D_HEADS
    dv = v.shape[2] // D_HEADS
    assert seq % c == 0 and (seq == c or valid_len == c)
    n0 = n0.reshape(b, D_HEADS, 1, dk)
    m0 = jnp.broadcast_to(m0.reshape(b, D_HEADS, 1, 1), (b, D_HEADS, 1, LANES))
    kern = functools.partial(_mlstm_kernel, c=c, dk=dk, dv=dv, valid_len=valid_len)

    def tok(w):
        return pl.BlockSpec((1, c, w), lambda i, j: (i, j, 0))

    c_spec = pl.BlockSpec((1, D_HEADS, dk, dv), lambda i, j: (i, 0, 0, 0))
    n_spec = pl.BlockSpec((1, D_HEADS, 1, dk), lambda i, j: (i, 0, 0, 0))
    m_spec = pl.BlockSpec((1, D_HEADS, 1, LANES), lambda i, j: (i, 0, 0, 0))
    y, c_out, n_out, m_out = pl.pallas_call(
        kern,
        grid=(b, seq // c),
        in_specs=[tok(D_HEADS * dk), tok(D_HEADS * dk), tok(D_HEADS * dv), tok(D_HEADS * dv), tok(LANES),
                  pl.BlockSpec((1, D_HEADS * dv), lambda i, j: (0, 0)), c_spec, n_spec, m_spec],
        out_specs=[tok(D_HEADS * dv), c_spec, n_spec, m_spec],
        out_shape=[jax.ShapeDtypeStruct((b, seq, D_HEADS * dv), BF16),
                   jax.ShapeDtypeStruct((b, D_HEADS, dk, dv), F32),
                   jax.ShapeDtypeStruct((b, D_HEADS, 1, dk), F32),
                   jax.ShapeDtypeStruct((b, D_HEADS, 1, LANES), F32)],
        compiler_params=_cparams("parallel", "arbitrary"),
        name="mlstm_recurrence",
    )(q, k, v, og, gates, norm_g.reshape(1, D_HEADS * dv), c0, n0, m0)
    return y, c_out, n_out.reshape(b, D_HEADS, dk), m_out[:, :, 0, 0]


def _ident(z):
    return (z,)


def _gelu(z):
    return (jax.nn.gelu(z, approximate=True),)


def _mix_gmlp(x, g, j, w_in, ln_g, ln_b, w_s, b_s, w_out):
    b, seq, d = x.shape
    width = w_in.shape[2] // 2
    chunk = w_s.shape[1]
    c = chunk if seq % chunk == 0 else seq
    x2 = x.reshape(b * seq, d)
    (z,) = _norm_matmul(x2, g, w_in, j, 0, 2 * width, _gelu, (F32,))
    a, v = _gmlp_gate(z.reshape(b * seq // c, c, 2 * width), ln_g, ln_b, w_s, b_s, c)
    y = _matmul_residual(a.reshape(b * seq, width), w_out, j, x2)
    return y.reshape(b, seq, d), v.reshape(b, seq, width)


def _moba_qkv(x, g, j, w_qkv):
    b, seq, d = x.shape
    hdm = w_qkv.shape[2] // 3
    x2 = x.reshape(b * seq, d)
    return [_norm_matmul(x2, g, w_qkv, j, i * hdm, hdm, _ident, (F32,))[0].reshape(b, seq, hdm) for i in range(3)]


def _hgrn_project(x2, g, j, w_in, lb):
    hdm = w_in.shape[2] // 4
    lb = lb.reshape(1, hdm)

    def silu(z):
        return (z * _sigmoid(z),)

    def forget(z, lbv):
        logf = jnp.log(lbv + (1.0 - lbv) * _sigmoid(z))
        return logf, (1.0 - lbv) * _sigmoid(-z)

    (q,) = _norm_matmul(x2, g, w_in, j, 0, hdm, silu, (F32,))
    logf, k = _norm_matmul(x2, g, w_in, j, hdm, hdm, forget, (F32, F32), extras=(lb,))
    (iv,) = _norm_matmul(x2, g, w_in, j, 2 * hdm, hdm, _ident, (F32,))
    (gate,) = _norm_matmul(x2, g, w_in, j, 3 * hdm, hdm, silu, (F32,))
    return q, k, iv, logf, gate


def _mix_hgrn(x, g, j, w_in, lb, norm_g, w_out, s0):
    b, seq, d = x.shape
    x2 = x.reshape(b * seq, d)
    parts = [p.reshape(b, seq, -1) for p in _hgrn_project(x2, g, j, w_in, lb)]
    if seq % 256 == 0:
        t_step, cs, pad = 256, 64, 0
    else:
        t_step = cs = -(-seq // SUB) * SUB
        pad = t_step - seq
        parts = [jnp.pad(p, ((0, 0), (0, pad), (0, 0))) for p in parts]
    y, s = _hgrn_recurrence(*parts, norm_g, s0, t_step, cs)
    y = y[:, :seq].reshape(b * seq, -1)
    return _matmul_residual(y, w_out, j, x2).reshape(b, seq, d), s


def _mix_mlstm(x, g, j, w_in, w_gates, b_gates, norm_g, w_out, c0, n0, m0):
    b, seq, d = x.shape
    dk = c0.shape[2]
    dv = c0.shape[3]
    nq, nv = D_HEADS * dk, D_HEADS * dv
    x2 = x.reshape(b * seq, d)

    def kscale(z):
        return (z * (dk ** -0.5),)

    def ogate(z):
        return (_sigmoid(z),)

    def gates_fn(z, bias):
        zz = z + bias
        lane = lax.broadcasted_iota(jnp.int32, zz.shape, 1)
        return (jnp.where(lane < D_HEADS, zz, _log_sigmoid(zz)),)

    (q,) = _norm_matmul(x2, g, w_in, j, 0, nq, _ident, (F32,))
    (k,) = _norm_matmul(x2, g, w_in, j, nq, nq, kscale, (F32,))
    (v,) = _norm_matmul(x2, g, w_in, j, 2 * nq, nv, _ident, (F32,))
    (og,) = _norm_matmul(x2, g, w_in, j, 2 * nq + nv, nv, ogate, (F32,))
    (gt,) = _norm_matmul(x2, g, w_gates, 0, 0, LANES, gates_fn, (F32,), extras=(b_gates,))
    parts = [p.reshape(b, seq, -1) for p in (q, k, v, og, gt)]
    if seq % 256 == 0:
        c, valid = 256, 256
    else:
        c = -(-seq // SUB) * SUB
        valid = seq
        parts = [jnp.pad(p, ((0, 0), (0, c - seq), (0, 0))) for p in parts]
    y, c_out, n_out, m_out = _mlstm_recurrence(*parts, norm_g, c0, n0, m0, c, valid)
    y = y[:, :seq].reshape(b * seq, nv)
    return _matmul_residual(y, w_out, j, x2).reshape(b, seq, d), c_out, n_out, m_out


def kernel(x_prompt, x_sample, cache_k, cache_v, page_table, state_hgrn, state_mlstm_c, state_mlstm_n, state_mlstm_m, norm_mix, norm_ffn, norm_final, w_ffn_up, w_ffn_down, a_w_in, a_ln_g, a_ln_b, a_w_s, a_b_s, a_w_out, b_w_qkv, b_w_out, c_w_in, c_lower_bound, c_norm_g, c_w_out, d_w_in, d_b_gates, d_norm_g, d_w_out):
    depth = norm_mix.shape[0]
    bsz, seq, d = x_prompt.shape
    dbsz, dseq, _ = x_sample.shape
    slopes = jnp.asarray(2.0 ** (-8.0 * np.arange(1, B_HEADS + 1) / B_HEADS), F32)
    lbs = jax.nn.softmax(c_lower_bound.astype(F32), axis=0)
    lbs = jnp.cumsum(lbs, axis=0) - lbs[0]
    hdm = b_w_qkv.shape[2] // 3

    xp, xs = x_prompt, x_sample
    outs = {k: [] for k in ("av", "kp", "vp", "ks", "vs", "hp", "hs", "cp", "np", "mp", "cs", "ns", "ms")}
    for layer in range(depth):
        kind = layer % N_MIXERS
        j = layer // N_MIXERS
        g = norm_mix[layer]
        if kind == 0:
            args = (j, a_w_in, a_ln_g[j], a_ln_b[j], a_w_s[j], a_b_s[j], a_w_out)
            xp, _ = _mix_gmlp(xp, g, *args)
            xs, vrow = _mix_gmlp(xs, g, *args)
            outs["av"].append(vrow)
        elif kind == 1:
            qp, kp, vp = _moba_qkv(xp, g, j, b_w_qkv)
            op = _moba_prompt(qp, kp, vp, slopes)
            xp = _matmul_residual(op.reshape(bsz * seq, hdm), b_w_out, j,
                                  xp.reshape(bsz * seq, d)).reshape(bsz, seq, d)
            qs, ks, vs = _moba_qkv(xs, g, j, b_w_qkv)
            os_ = _moba_sample(qs, ks, vs, cache_k, cache_v, j, page_table, slopes)
            xs = _matmul_residual(os_.reshape(dbsz * dseq, hdm), b_w_out, j,
                                  xs.reshape(dbsz * dseq, d)).reshape(dbsz, dseq, d)
            hd = hdm // B_HEADS
            outs["kp"].append(kp.reshape(bsz, seq, B_HEADS, hd))
            outs["vp"].append(vp.reshape(bsz, seq, B_HEADS, hd))
            outs["ks"].append(ks.reshape(dbsz, dseq, B_HEADS, hd))
            outs["vs"].append(vs.reshape(dbsz, dseq, B_HEADS, hd))
        elif kind == 2:
            args = (j, c_w_in, lbs[layer], c_norm_g[j], c_w_out)
            s0 = jnp.zeros((bsz,) + state_hgrn.shape[2:], F32)
            xp, sp = _mix_hgrn(xp, g, *args, s0)
            xs, ss = _mix_hgrn(xs, g, *args, state_hgrn[j])
            outs["hp"].append(sp)
            outs["hs"].append(ss)
        else:
            ng = 2 * D_HEADS * (state_mlstm_c.shape[3] + state_mlstm_c.shape[4])
            w_gates = jnp.pad(d_w_in[j][:, ng:], ((0, 0), (0, LANES - 2 * D_HEADS)))[None]
            b_gates = jnp.pad(d_b_gates[j], (0, LANES - 2 * D_HEADS)).reshape(1, LANES)
            args = (j, d_w_in, w_gates, b_gates, d_norm_g[j], d_w_out)
            zc = jnp.zeros((bsz,) + state_mlstm_c.shape[2:], F32)
            zn = jnp.zeros((bsz,) + state_mlstm_n.shape[2:], F32)
            zm = jnp.zeros((bsz,) + state_mlstm_m.shape[2:], F32)
            xp, cp, np_, mp = _mix_mlstm(xp, g, *args, zc, zn, zm)
            xs, cs, ns, ms = _mix_mlstm(xs, g, *args, state_mlstm_c[j], state_mlstm_n[j], state_mlstm_m[j])
            for key, val in zip(("cp", "np", "mp", "cs", "ns", "ms"), (cp, np_, mp, cs, ns, ms)):
                outs[key].append(val)
        xp = _ffn(xp.reshape(bsz * seq, d), norm_ffn[layer], w_ffn_up, w_ffn_down, layer).reshape(bsz, seq, d)
        xs = _ffn(xs.reshape(dbsz * dseq, d), norm_ffn[layer], w_ffn_up, w_ffn_down, layer).reshape(dbsz, dseq, d)
    y_prompt = _rmsnorm(xp.reshape(bsz * seq, d), norm_final).reshape(bsz, seq, d)
    y_sample = _rmsnorm(xs.reshape(dbsz * dseq, d), norm_final).reshape(dbsz, dseq, d)
    st = {k: jnp.stack(v) for k, v in outs.items()}
    return (y_prompt, y_sample, st["av"], st["kp"], st["vp"], st["ks"], st["vs"], st["hp"], st["hs"],
            st["cp"], st["np"], st["mp"], st["cs"], st["ns"], st["ms"])
```

```python
import functools

import jax
import jax.numpy as jnp
import numpy as np
from jax import lax
from jax.experimental import pallas as pl
from jax.experimental.pallas import tpu as pltpu

F32 = jnp.float32
BF16 = jnp.bfloat16
EPS = 1e-6
NEG_INF = float("-inf")

A_GROUPS = 8
B_HEADS = 16
B_BLOCK = 256
B_TOPK = 3
C_HEADS = 16
C_DK = 128
D_HEADS = 4
N_MIXERS = 4

LANES = 128
SUBLANES = 8
VMEM_LIMIT_BYTES = 56 * 1024 * 1024
SUB = 16
MM_TILE_M = 1024
MM_TILE_N = 512
FFN_TILE_F = 512
RES_TILE_N = 1024


def _cparams(*sem):
    return pltpu.CompilerParams(dimension_semantics=sem, vmem_limit_bytes=VMEM_LIMIT_BYTES)


def _sigmoid(x):
    return 1.0 / (1.0 + jnp.exp(-x))


def _log_sigmoid(x):
    return jnp.minimum(x, 0.0) - jnp.log(1.0 + jnp.exp(-jnp.abs(x)))


def _split3(x):
    hi = x.astype(BF16)
    r1 = x - hi.astype(F32)
    mid = r1.astype(BF16)
    lo = (r1 - mid.astype(F32)).astype(BF16)
    return hi, mid, lo


def _tri_cumsum(x, block):
    n = x.shape[0]
    r = lax.broadcasted_iota(jnp.int32, (n, n), 0)
    c = lax.broadcasted_iota(jnp.int32, (n, n), 1)
    tri = jnp.where((c <= r) & ((r // block) == (c // block)), 1.0, 0.0).astype(BF16)
    hi, mid, lo = _split3(x)
    out = jnp.dot(tri, lo, preferred_element_type=F32)
    out = out + jnp.dot(tri, mid, preferred_element_type=F32)
    return out + jnp.dot(tri, hi, preferred_element_type=F32)


def _dot_nt(a, b):
    return lax.dot_general(a, b, (((1,), (1,)), ((), ())), preferred_element_type=F32)


def _dot_tn(a, b):
    return lax.dot_general(a, b, (((0,), (0,)), ((), ())), preferred_element_type=F32)


def _dot_nt_f32(a, b):
    a0, a1, a2 = _split3(a)
    b0, b1, b2 = _split3(b)
    out = _dot_nt(a1, b1) + _dot_nt(a0, b2) + _dot_nt(a2, b0)
    out = out + _dot_nt(a0, b1) + _dot_nt(a1, b0)
    return out + _dot_nt(a0, b0)


def _norm_mm_kernel(x_ref, g_ref, w_ref, *rest, parts):
    n_extra = sum(p[3] for p in parts)
    n_out = sum(p[4] for p in parts)
    extras, outs, xn_ref = rest[:n_extra], rest[n_extra:n_extra + n_out], rest[n_extra + n_out]
    j = pl.program_id(1)

    @pl.when(j == 0)
    def _():
        x = x_ref[...]
        y = x * lax.rsqrt(jnp.mean(x * x, axis=-1, keepdims=True) + EPS)
        xn_ref[...] = (y * g_ref[...]).astype(BF16)

    z = jnp.dot(xn_ref[...], w_ref[...].astype(BF16), preferred_element_type=F32)
    e0 = o0 = 0
    for jb0, nb, epilogue, ne, no in parts:
        def finish(epilogue=epilogue, ex=extras[e0:e0 + ne], os=outs[o0:o0 + no]):
            for o_ref, r in zip(os, epilogue(z, *[e[...] for e in ex])):
                o_ref[...] = r.astype(o_ref.dtype)

        if len(parts) == 1:
            finish()
        else:
            pl.when((j >= jb0) & (j < jb0 + nb))(finish)
        e0, o0 = e0 + ne, o0 + no


def _norm_matmul_parts(x, g, w, layer, col0, parts):
    m, d = x.shape
    tm = min(m, MM_TILE_M)
    tn = min(min(p[0] for p in parts), MM_TILE_N)
    assert m % tm == 0 and col0 % tn == 0 and all(p[0] % tn == 0 for p in parts)
    jw = col0 // tn
    kparts, extra_specs, out_specs, out_shapes, extra_args = [], [], [], [], []
    jb0 = 0
    for ncols, epilogue, out_dtypes, extras in parts:
        nb = ncols // tn

        def col_block(i, j, jb0=jb0, nb=nb):
            return jnp.clip(j - jb0, 0, nb - 1)

        kparts.append((jb0, nb, epilogue, len(extras), len(out_dtypes)))
        extra_specs += [pl.BlockSpec((1, tn), lambda i, j, cb=col_block: (0, cb(i, j))) for _ in extras]
        out_specs += [pl.BlockSpec((tm, tn), lambda i, j, cb=col_block: (i, cb(i, j))) for _ in out_dtypes]
        out_shapes += [jax.ShapeDtypeStruct((m, ncols), dt) for dt in out_dtypes]
        extra_args += list(extras)
        jb0 += nb
    res = pl.pallas_call(
        functools.partial(_norm_mm_kernel, parts=tuple(kparts)),
        grid=(m // tm, jb0),
        in_specs=[pl.BlockSpec((tm, d), lambda i, j: (i, 0)),
                  pl.BlockSpec((1, d), lambda i, j: (0, 0)),
                  pl.BlockSpec((None, d, tn), lambda i, j: (layer, 0, j + jw))] + extra_specs,
        out_specs=out_specs,
        out_shape=out_shapes,
        scratch_shapes=[pltpu.VMEM((tm, d), BF16)],
        compiler_params=_cparams("parallel", "arbitrary"),
        name="norm_matmul",
    )(x, g.reshape(1, d), w, *extra_args)
    out, o0 = [], 0
    for p in parts:
        out.append(tuple(res[o0:o0 + len(p[2])]))
        o0 += len(p[2])
    return out


def _norm_matmul(x, g, w, layer, col0, ncols, epilogue, out_dtypes, extras=()):
    return _norm_matmul_parts(x, g, w, layer, col0, [(ncols, epilogue, out_dtypes, extras)])[0]


def _mm_res_kernel(a_ref, w_ref, r_ref, o_ref):
    o_ref[...] = r_ref[...] + jnp.dot(a_ref[...].astype(BF16), w_ref[...].astype(BF16),
                                      preferred_element_type=F32)


def _matmul_residual(a, w, layer, res):
    m, k = a.shape
    n = w.shape[2]
    tm = min(m, MM_TILE_M)
    tn = min(n, RES_TILE_N)
    assert m % tm == 0 and n % tn == 0
    return pl.pallas_call(
        _mm_res_kernel,
        grid=(m // tm, n // tn),
        in_specs=[pl.BlockSpec((tm, k), lambda i, j: (i, 0)),
                  pl.BlockSpec((None, k, tn), lambda i, j: (layer, 0, j)),
                  pl.BlockSpec((tm, tn), lambda i, j: (i, j))],
        out_specs=pl.BlockSpec((tm, tn), lambda i, j: (i, j)),
        out_shape=jax.ShapeDtypeStruct((m, n), F32),
        compiler_params=_cparams("parallel", "parallel"),
        name="matmul_residual",
    )(a, w, res)


def _ffn_kernel(x_ref, g_ref, wu_ref, wd_ref, o_ref, xn_ref):
    f = pl.program_id(1)

    @pl.when(f == 0)
    def _():
        x = x_ref[...]
        y = x * lax.rsqrt(jnp.mean(x * x, axis=-1, keepdims=True) + EPS)
        xn_ref[...] = (y * g_ref[...]).astype(BF16)
        o_ref[...] = x

    h = jnp.maximum(jnp.dot(xn_ref[...], wu_ref[...].astype(BF16), preferred_element_type=F32), 0.0)
    o_ref[...] += jnp.dot((h * h).astype(BF16), wd_ref[...].astype(BF16), preferred_element_type=F32)


def _ffn(x, g, w_up, w_down, layer):
    m, d = x.shape
    dff = w_up.shape[2]
    tm = min(m, MM_TILE_M)
    tf = min(dff, FFN_TILE_F)
    assert m % tm == 0 and dff % tf == 0
    return pl.pallas_call(
        _ffn_kernel,
        grid=(m // tm, dff // tf),
        in_specs=[pl.BlockSpec((tm, d), lambda i, f: (i, 0), pipeline_mode=pl.Buffered(1)),
                  pl.BlockSpec((1, d), lambda i, f: (0, 0)),
                  pl.BlockSpec((None, d, tf), lambda i, f: (layer, 0, f)),
                  pl.BlockSpec((None, tf, d), lambda i, f: (layer, f, 0))],
        out_specs=pl.BlockSpec((tm, d), lambda i, f: (i, 0)),
        out_shape=jax.ShapeDtypeStruct((m, d), F32),
        scratch_shapes=[pltpu.VMEM((tm, d), BF16)],
        compiler_params=_cparams("parallel", "arbitrary"),
        name="ffn",
    )(x, g.reshape(1, d), w_up, w_down)


def _rmsnorm_kernel(x_ref, g_ref, o_ref):
    x = x_ref[...]
    o_ref[...] = x * lax.rsqrt(jnp.mean(x * x, axis=-1, keepdims=True) + EPS) * g_ref[...]


def _rmsnorm(x, g):
    m, d = x.shape
    tm = min(m, 512)
    return pl.pallas_call(
        _rmsnorm_kernel,
        grid=(m // tm,),
        in_specs=[pl.BlockSpec((tm, d), lambda i: (i, 0)), pl.BlockSpec((1, d), lambda i: (0, 0))],
        out_specs=pl.BlockSpec((tm, d), lambda i: (i, 0)),
        out_shape=jax.ShapeDtypeStruct((m, d), F32),
        compiler_params=_cparams("parallel"),
        name="final_rmsnorm",
    )(x, g.reshape(1, d))


def _gmlp_gate_kernel(z_ref, lg_ref, lb_ref, ws_ref, bs_ref, a_ref, v_ref, *, c, width):
    gw = width // A_GROUPS
    z = z_ref[0]
    u = z[:, :width]
    vr = z[:, width:]
    mu = jnp.mean(vr, axis=-1, keepdims=True)
    vc = vr - mu
    v = vc * lax.rsqrt(jnp.mean(vc * vc, axis=-1, keepdims=True) + EPS) * lg_ref[...] + lb_ref[...]
    v_ref[0] = v
    row = lax.broadcasted_iota(jnp.int32, (c, c), 0)
    col = lax.broadcasted_iota(jnp.int32, (c, c), 1)
    bs = bs_ref[...]
    for g in range(A_GROUPS):
        ws = jnp.where(col <= row, ws_ref[g], 0.0)
        vg = v[:, g * gw:(g + 1) * gw]
        if c >= 2 * SUBLANES:
            s = jnp.dot(ws.astype(BF16), vg.astype(BF16), preferred_element_type=F32)
        else:
            s = jnp.zeros((c, gw), F32)
            for t in range(c):
                s = s + ws[:, t:t + 1] * vg[t:t + 1, :]
        s = s + bs[:, g:g + 1]
        a_ref[0, :, g * gw:(g + 1) * gw] = (u[:, g * gw:(g + 1) * gw] * s).astype(a_ref.dtype)


def _gmlp_gate(z, ln_g, ln_b, w_s, b_s, c):
    nb, _, w2 = z.shape
    width = w2 // 2
    ws = w_s[:, :c, :c]
    bs_t = b_s[:, :c].T
    kern = functools.partial(_gmlp_gate_kernel, c=c, width=width)
    return pl.pallas_call(
        kern,
        grid=(nb,),
        in_specs=[pl.BlockSpec((1, c, w2), lambda i: (i, 0, 0)),
                  pl.BlockSpec((1, width), lambda i: (0, 0)),
                  pl.BlockSpec((1, width), lambda i: (0, 0)),
                  pl.BlockSpec((A_GROUPS, c, c), lambda i: (0, 0, 0)),
                  pl.BlockSpec((c, A_GROUPS), lambda i: (0, 0))],
        out_specs=[pl.BlockSpec((1, c, width), lambda i: (i, 0, 0)),
                   pl.BlockSpec((1, c, width), lambda i: (i, 0, 0))],
        out_shape=[jax.ShapeDtypeStruct((nb, c, width), BF16 if c % (2 * SUBLANES) == 0 else F32),
                   jax.ShapeDtypeStruct((nb, c, width), F32)],
        compiler_params=_cparams("parallel"),
        name="gmlp_gate",
    )(z, ln_g.reshape(1, width), ln_b.reshape(1, width), ws, bs_t)


MASKED = -1e30
V_PAD_ROWS = 16


MOBA_HEADS_PER_STEP = 2


def _moba_prompt_kernel(slope_ref, q_ref, k_ref, v_ref, o_ref, kaug_ref, vt_ref, means_ref, bias_ref,
                        *, nblk, hd, hps):
    blk = B_BLOCK
    scale = hd ** -0.5
    nbp = means_ref.shape[1]
    heads = range(hps)
    slopes = [slope_ref[pl.program_id(1) * hps + hh] for hh in heads]

    lane_blk = lax.broadcasted_iota(jnp.int32, (blk, hd), 1)
    ones_rows = jnp.where(lax.broadcasted_iota(jnp.int32, (V_PAD_ROWS, blk), 0) == 0, 1.0, 0.0).astype(BF16)
    key_i = lax.broadcasted_iota(jnp.int32, (blk, blk), 0)
    qry_i = lax.broadcasted_iota(jnp.int32, (blk, blk), 1)
    dist = (qry_i - key_i).astype(F32)
    means_ref[...] = jnp.zeros_like(means_ref)
    for hh in heads:
        cols = slice(hh * hd, (hh + 1) * hd)
        for n in range(nblk):
            kb = k_ref[0, n * blk:(n + 1) * blk, cols]
            means_ref[hh, n:n + 1, :] = jnp.sum(kb, axis=0, keepdims=True) * (1.0 / blk)
            kaug_ref[hh, n * blk:(n + 1) * blk, 0:hd] = kb.astype(BF16)
            kaug_ref[hh, n * blk:(n + 1) * blk, hd:2 * hd] = jnp.where(lane_blk == n, 1.0, 0.0).astype(BF16)
            vt_ref[hh, n, 0:hd, :] = v_ref[0, n * blk:(n + 1) * blk, cols].T.astype(BF16)
            vt_ref[hh, n, hd:hd + V_PAD_ROWS, :] = ones_rows
        bias_ref[hh, 0:blk, :] = -slopes[hh] * (dist + blk)
        bias_ref[hh, blk:2 * blk, :] = -slopes[hh] * dist
        bias_ref[hh, 2 * blk:3 * blk, :] = jnp.where(dist >= 0, -slopes[hh] * dist, NEG_INF)
    mean_parts = [_split3(means_ref[hh])[:2] for hh in heads]
    rown = lax.broadcasted_iota(jnp.int32, (nbp, blk), 0)

    def update(st, s, c, vt):
        m, acc = st
        m_new = jnp.maximum(m, jnp.max(s, axis=0, keepdims=True) + c)
        p = jnp.exp(s - (m_new - c))
        return m_new, jnp.exp(m - m_new) * acc + jnp.dot(vt, p.astype(BF16), preferred_element_type=F32)

    def q_block(qb, carry):
        q0 = pl.multiple_of(qb * blk, blk)
        qaugs, state = [], []
        for hh in heads:
            q = q_ref[0, pl.ds(q0, blk), hh * hd:(hh + 1) * hd]
            q_hi, q_mid, _ = _split3(q)
            mean_hi, mean_mid = mean_parts[hh]
            sc = _dot_nt(mean_hi, q_mid) + _dot_nt(mean_mid, q_hi) + _dot_nt(mean_hi, q_hi)
            sc = jnp.where(rown < qb, sc, NEG_INF)
            cnt = jnp.zeros((nbp, blk), F32)
            for n2 in range(nblk):
                rn = sc[n2:n2 + 1, :]
                ahead = (rn > sc) | ((rn == sc) & (rown > n2))
                cnt = cnt + jnp.where(ahead, 1.0, 0.0)
            keep = ((cnt < B_TOPK) & (rown < qb)) | (rown == qb)
            selb = jnp.where(keep, 0.0, MASKED)
            qaug = jnp.concatenate([(q * scale).T, selb, jnp.zeros((hd - nbp, blk), F32)], axis=0).astype(BF16)
            s = jnp.dot(kaug_ref[hh, pl.ds(q0, blk), :], qaug, preferred_element_type=F32) \
                + bias_ref[hh, 2 * blk:3 * blk, :]
            m = jnp.max(s, axis=0, keepdims=True)
            p = jnp.exp(s - m)
            acc = jnp.dot(vt_ref[hh, qb], p.astype(BF16), preferred_element_type=F32)
            qaugs.append(qaug)
            state.append((m, acc))

        def one_block(st):
            out = []
            for hh in heads:
                c = -slopes[hh] * lax.convert_element_type(qb * blk, F32)
                s = jnp.dot(kaug_ref[hh, 0:blk, :], qaugs[hh], preferred_element_type=F32) \
                    + bias_ref[hh, blk:2 * blk, :]
                out.append(update(st[hh], s, c, vt_ref[hh, 0]))
            return tuple(out)

        def two_blocks(i, st):
            n = odd + 2 * i
            k0 = pl.multiple_of(n * blk, blk)
            out = []
            for hh in heads:
                c = -slopes[hh] * lax.convert_element_type((qb - n - 1) * blk, F32)
                s = jnp.dot(kaug_ref[hh, pl.ds(k0, 2 * blk), :], qaugs[hh], preferred_element_type=F32) \
                    + bias_ref[hh, 0:2 * blk, :]
                out.append(update(st[hh], s, c, jnp.concatenate([vt_ref[hh, n], vt_ref[hh, n + 1]], axis=1)))
            return tuple(out)

        odd = qb % 2
        state = lax.cond(odd == 1, one_block, lambda st: st, tuple(state))
        state = lax.fori_loop(0, qb // 2, two_blocks, state)
        for hh in heads:
            acc = state[hh][1]
            o_t = acc[0:hd, :] / acc[hd:hd + 1, :]
            o_ref[0, pl.ds(q0, blk), hh * hd:(hh + 1) * hd] = o_t.T.astype(o_ref.dtype)
        return carry

    lax.fori_loop(0, nblk, q_block, 0)


def _moba_prompt(q, k, v, slopes):
    b, seq, hdm = q.shape
    hd = hdm // B_HEADS
    hps = MOBA_HEADS_PER_STEP
    assert seq % B_BLOCK == 0 and hd == LANES and B_HEADS % hps == 0
    nblk = seq // B_BLOCK
    nbp = -(-nblk // SUBLANES) * SUBLANES
    assert nbp <= hd and nblk >= 2
    kern = functools.partial(_moba_prompt_kernel, nblk=nblk, hd=hd, hps=hps)
    spec = pl.BlockSpec((1, seq, hps * hd), lambda i, h: (i, 0, h))
    return pl.pallas_call(
        kern,
        grid=(b, B_HEADS // hps),
        in_specs=[pl.BlockSpec(memory_space=pltpu.SMEM), spec, spec, spec],
        out_specs=spec,
        out_shape=jax.ShapeDtypeStruct((b, seq, hdm), BF16),
        scratch_shapes=[pltpu.VMEM((hps, seq, 2 * hd), BF16),
                        pltpu.VMEM((hps, nblk, hd + V_PAD_ROWS, B_BLOCK), BF16),
                        pltpu.VMEM((hps, nbp, hd), F32),
                        pltpu.VMEM((hps, 3 * B_BLOCK, B_BLOCK), F32)],
        compiler_params=_cparams("parallel", "parallel"),
        name="moba_prompt",
    )(slopes, q, k, v)


MEAN_PAGES_PER_STEP = 8


def _page_mean_kernel(pt_ref, *refs, ppb, pps):
    o_ref = refs[pps]
    for blk in range(pps // ppb):
        s = jnp.sum(refs[blk * ppb][0, 0], axis=0)
        for pg in range(1, ppb):
            s = s + jnp.sum(refs[blk * ppb + pg][0, 0], axis=0)
        o_ref[0, blk] = s * (1.0 / B_BLOCK)


def _cache_block_means(cache, layer, page_table):
    _, npool, page, heads, hd = cache.shape
    db, npg = page_table.shape
    ppb = B_BLOCK // page
    nblk = npg // ppb
    pps = MEAN_PAGES_PER_STEP
    while npg % pps:
        pps //= 2
    assert pps % ppb == 0
    kern = functools.partial(_page_mean_kernel, ppb=ppb, pps=pps)

    def page_spec(r):
        return pl.BlockSpec((1, 1, page, heads, hd), lambda i, p, pt: (layer, pt[i, p * pps + r], 0, 0, 0))

    return pl.pallas_call(
        kern,
        grid_spec=pltpu.PrefetchScalarGridSpec(
            num_scalar_prefetch=1,
            grid=(db, npg // pps),
            in_specs=[page_spec(r) for r in range(pps)],
            out_specs=pl.BlockSpec((1, pps // ppb, heads, hd), lambda i, p, pt: (i, p, 0, 0)),
        ),
        out_shape=jax.ShapeDtypeStruct((db, nblk, heads, hd), F32),
        compiler_params=_cparams("parallel", "parallel"),
        name="moba_cache_means",
    )(page_table, *([cache] * pps))


def _moba_select_kernel(q_ref, mean_ref, o_ref, *, s_len, hd, n_valid):
    rows, hdm = q_ref.shape[1], q_ref.shape[2]
    nblk = mean_ref.shape[1]
    r_i = lax.broadcasted_iota(jnp.int32, (rows, hdm), 0)
    c_i = lax.broadcasted_iota(jnp.int32, (rows, hdm), 1)
    qx = jnp.where((c_i // hd) == (r_i // s_len), q_ref[0], 0.0)
    sc = _dot_nt_f32(qx, mean_ref[0])
    lane = lax.broadcasted_iota(jnp.int32, (rows, nblk), 1).astype(F32)
    out_lane = lax.broadcasted_iota(jnp.int32, (rows, LANES), 1)
    out = jnp.zeros((rows, LANES), F32)
    for t in range(n_valid):
        mx = jnp.max(sc, axis=-1, keepdims=True)
        idx = jnp.min(jnp.where(sc == mx, lane, float(nblk)), axis=-1, keepdims=True)
        out = jnp.where(out_lane == t, idx, out)
        sc = jnp.where(lane == idx, NEG_INF, sc)
    o_ref[0] = out.astype(jnp.int32)


def _moba_select(q_rep, means, s_len, n_valid):
    db, rows, hdm = q_rep.shape
    nblk = means.shape[1]
    kern = functools.partial(_moba_select_kernel, s_len=s_len, hd=hdm // B_HEADS, n_valid=n_valid)
    return pl.pallas_call(
        kern,
        grid=(db,),
        in_specs=[pl.BlockSpec((1, rows, hdm), lambda i: (i, 0, 0)),
                  pl.BlockSpec((1, nblk, hdm), lambda i: (i, 0, 0))],
        out_specs=pl.BlockSpec((1, rows, LANES), lambda i: (i, 0, 0)),
        out_shape=jax.ShapeDtypeStruct((db, rows, LANES), jnp.int32),
        compiler_params=_cparams("parallel"),
        name="moba_sample_select",
    )(q_rep, means)


def _moba_sample_kernel(phys_ref, idx_ref, slope_ref, q_ref, kn_ref, vn_ref, ck_hbm, cv_hbm, o_ref,
                        kbuf, vbuf, sems, q8_ref, *, layer, s_len, hd, past, n_valid, nsel, ppb):
    npages = nsel * ppb
    g = pl.program_id(0)
    h = g % B_HEADS
    slot = g % 2
    slope = slope_ref[h]
    scale = hd ** -0.5
    page = kbuf.shape[2]
    row8 = lax.broadcasted_iota(jnp.int32, (SUBLANES, 1), 0)

    def page_copies(step, to_slot):
        head = step % B_HEADS
        out = []
        for r in range(npages):
            pool_row = phys_ref[step * npages + r]
            out.append(pltpu.make_async_copy(ck_hbm.at[layer, pool_row, :, head, :], kbuf.at[to_slot, r],
                                             sems.at[0, to_slot]))
            out.append(pltpu.make_async_copy(cv_hbm.at[layer, pool_row, :, head, :], vbuf.at[to_slot, r],
                                             sems.at[1, to_slot]))
        return out

    @pl.when(g == 0)
    def _():
        for cp in page_copies(g, slot):
            cp.start()

    @pl.when(g + 1 < pl.num_programs(0))
    def _():
        for cp in page_copies(g + 1, 1 - slot):
            cp.start()

    for cp in page_copies(g, slot):
        cp.wait()
    k_refs = [kbuf.at[slot, r] for r in range(npages)]
    v_refs = [vbuf.at[slot, r] for r in range(npages)]

    q8_ref[...] = jnp.zeros_like(q8_ref)
    q8_ref[0:s_len, :] = q_ref[0] * scale
    q8 = q8_ref[...]
    kn = kn_ref[0]
    vn = vn_ref[0]
    s_cols = []
    for c in range(s_len):
        sc = jnp.sum(q8 * kn[c:c + 1, :], axis=-1, keepdims=True) - slope * (row8 - c).astype(F32)
        s_cols.append(jnp.where(row8 >= c, sc, NEG_INF))
    m = s_cols[0]
    for c in range(1, s_len):
        m = jnp.maximum(m, s_cols[c])

    qs = q8.astype(BF16)
    col = lax.broadcasted_iota(jnp.int32, (SUBLANES, ppb * page), 1)
    row = lax.broadcasted_iota(jnp.int32, (SUBLANES, ppb * page), 0)
    live = [j for j in range(nsel) if j % B_TOPK < n_valid]
    tiles = []
    for j in live:
        blk_id = idx_ref[g * nsel + j]
        kb = jnp.concatenate([k_refs[j * ppb + pg][...] for pg in range(ppb)], axis=0).astype(BF16)
        dist = (past + row - blk_id * B_BLOCK - col).astype(F32)
        s = jnp.where(row == j // B_TOPK, _dot_nt(qs, kb) - slope * dist, NEG_INF)
        tiles.append(s)
        m = jnp.maximum(m, jnp.max(s, axis=-1, keepdims=True))

    l = jnp.zeros((SUBLANES, 1), F32)
    acc = jnp.zeros((SUBLANES, hd), F32)
    for c in range(s_len):
        p = jnp.exp(s_cols[c] - m)
        l = l + p
        acc = acc + p * vn[c:c + 1, :]
    for j, s in zip(live, tiles):
        p = jnp.exp(s - m)
        vb = jnp.concatenate([v_refs[j * ppb + pg][...] for pg in range(ppb)], axis=0).astype(BF16)
        l = l + jnp.sum(p, axis=-1, keepdims=True)
        acc = acc + jnp.dot(p.astype(BF16), vb, preferred_element_type=F32)
    o_ref[0] = (acc / l)[0:s_len, :]


def _moba_sample(q, k_new, v_new, cache_k, cache_v, layer, page_table, slopes):
    db, s_len, hdm = q.shape
    hd = hdm // B_HEADS
    page = cache_k.shape[2]
    npg = page_table.shape[1]
    past = npg * page
    assert B_BLOCK % page == 0 and past % B_BLOCK == 0 and s_len <= SUBLANES
    own_blk = past // B_BLOCK
    assert own_blk >= 1
    n_valid = min(own_blk, B_TOPK)
    nsel = s_len * B_TOPK

    means = _cache_block_means(cache_k, layer, page_table).reshape(db, own_blk, hdm)
    q_rep = jnp.tile(q, (1, B_HEADS, 1))
    top = _moba_select(q_rep, means, s_len, n_valid)[:, :, :B_TOPK]
    top = jnp.minimum(top, own_blk - 1)
    idx = top.reshape(db, B_HEADS, nsel)
    ppb = B_BLOCK // page
    pages = idx[..., None] * ppb + jnp.arange(ppb, dtype=jnp.int32)
    phys = jnp.take_along_axis(page_table[:, None, :], pages.reshape(db, 1, -1), axis=2)
    phys = phys.reshape(-1).astype(jnp.int32)
    idx_flat = idx.reshape(-1).astype(jnp.int32)

    npages = nsel * ppb
    row_spec = pl.BlockSpec((1, s_len, hd), lambda g, ph, ix: (g // B_HEADS, 0, g % B_HEADS))
    kern = functools.partial(_moba_sample_kernel, layer=layer, s_len=s_len, hd=hd, past=past, n_valid=n_valid,
                             nsel=nsel, ppb=ppb)
    return pl.pallas_call(
        kern,
        grid_spec=pltpu.PrefetchScalarGridSpec(
            num_scalar_prefetch=2,
            grid=(db * B_HEADS,),
            in_specs=[pl.BlockSpec(memory_space=pltpu.SMEM), row_spec, row_spec, row_spec,
                      pl.BlockSpec(memory_space=pl.ANY), pl.BlockSpec(memory_space=pl.ANY)],
            out_specs=row_spec,
            scratch_shapes=[pltpu.VMEM((2, npages, page, hd), F32), pltpu.VMEM((2, npages, page, hd), F32),
                            pltpu.SemaphoreType.DMA((2, 2)), pltpu.VMEM((SUBLANES, hd), F32)],
        ),
        out_shape=jax.ShapeDtypeStruct((db, s_len, hdm), F32),
        compiler_params=_cparams("arbitrary"),
        name="moba_sample_attend",
    )(phys, idx_flat, slopes, q, k_new, v_new, cache_k, cache_v)


def _hgrn_kernel(q_ref, k_ref, i_ref, f_ref, g_ref, ng_ref, s0_ref, y_ref, s_ref, st_ref, gc_ref, o_ref,
                 *, t_step, cs, hps):
    c = pl.program_id(2)
    nsub = cs // SUB
    dk = dv = LANES

    @pl.when(c == 0)
    def _():
        for hh in range(hps):
            st_ref[hh] = s0_ref[0, hh].T

    gc_ref[...] = _tri_cumsum(f_ref[0], cs)
    rsub = lax.broadcasted_iota(jnp.int32, (SUB, 1), 0)

    def chunk(ci, carry):
        r0 = pl.multiple_of(ci * cs, cs)
        for hh in range(hps):
            q = q_ref[0, pl.ds(r0, cs), hh * dk:(hh + 1) * dk]
            k = k_ref[0, pl.ds(r0, cs), hh * dk:(hh + 1) * dk]
            iv = i_ref[0, pl.ds(r0, cs), hh * dv:(hh + 1) * dv]
            gcum = gc_ref[pl.ds(r0, cs), hh * dk:(hh + 1) * dk]
            st = st_ref[hh]
            st16 = st.astype(BF16)
            iv16 = iv.astype(BF16)
            o_inter = _dot_nt((q * jnp.exp(gcum)).astype(BF16), st16)
            for si in range(nsub):
                lo, hi = si * SUB, (si + 1) * SUB
                g_i = gcum[lo:hi]
                q_i = q[lo:hi]
                k_i = k[lo:hi]
                i_i = iv[lo:hi]
                acc = o_inter[lo:hi]
                if si > 0:
                    gref = gcum[lo - 1:lo]
                    qt = (q_i * jnp.exp(g_i - gref)).astype(BF16)
                    kt = (k[:lo] * jnp.exp(gref - gcum[:lo])).astype(BF16)
                    att = _dot_nt(qt, kt)
                    acc = acc + jnp.dot(att.astype(BF16), iv16[:lo], preferred_element_type=F32)
                for s in range(SUB):
                    e = jnp.exp(jnp.where(rsub >= s, g_i - g_i[s:s + 1], NEG_INF))
                    a = jnp.sum(q_i * k_i[s:s + 1] * e, axis=-1, keepdims=True)
                    acc = acc + a * i_i[s:s + 1]
                o_ref[pl.ds(r0 + lo, SUB), hh * dv:(hh + 1) * dv] = acc
            glast = gcum[cs - 1:cs]
            kh = (k * jnp.exp(glast - gcum)).astype(BF16)
            st_ref[hh] = st * jnp.exp(glast) + _dot_tn(iv16, kh)
        return carry

    lax.fori_loop(0, t_step // cs, chunk, 0)
    for hh in range(hps):
        o = o_ref[:, hh * dv:(hh + 1) * dv]
        y = o * lax.rsqrt(jnp.mean(o * o, axis=-1, keepdims=True) + EPS) * ng_ref[...]
        y_ref[0, :, hh * dv:(hh + 1) * dv] = (y * g_ref[0, :, hh * dv:(hh + 1) * dv]).astype(y_ref.dtype)

    @pl.when(c == pl.num_programs(2) - 1)
    def _():
        for hh in range(hps):
            s_ref[0, hh] = st_ref[hh].T


HGRN_HEADS_PER_STEP = 4


def _hgrn_recurrence(q, k, iv, logf, gate, norm_g, s0, t_step, cs):
    b, seq, hdm = q.shape
    dk = C_DK
    dv = hdm // C_HEADS
    hps = HGRN_HEADS_PER_STEP
    assert dk == LANES and dv == LANES and seq % t_step == 0 and t_step % cs == 0 and cs % SUB == 0
    assert C_HEADS % hps == 0
    kern = functools.partial(_hgrn_kernel, t_step=t_step, cs=cs, hps=hps)
    tok = pl.BlockSpec((1, t_step, hps * dv), lambda i, h, c: (i, c, h))
    st = pl.BlockSpec((1, hps, dk, dv), lambda i, h, c: (i, h, 0, 0))
    return pl.pallas_call(
        kern,
        grid=(b, C_HEADS // hps, seq // t_step),
        in_specs=[tok, tok, tok, tok, tok, pl.BlockSpec((1, dv), lambda i, h, c: (0, 0)), st],
        out_specs=[tok, st],
        out_shape=[jax.ShapeDtypeStruct((b, seq, hdm), BF16),
                   jax.ShapeDtypeStruct((b, C_HEADS, dk, dv), F32)],
        scratch_shapes=[pltpu.VMEM((hps, dv, dk), F32), pltpu.VMEM((t_step, hps * dk), F32),
                        pltpu.VMEM((t_step, hps * dv), F32)],
        compiler_params=_cparams("parallel", "parallel", "arbitrary"),
        name="hgrn_recurrence",
    )(q, k, iv, logf, gate, norm_g.reshape(1, dv), s0)


def _mlstm_kernel(q_ref, k_ref, v_ref, og_ref, gt_ref, ng_ref, c0_ref, n0_ref, m0_ref,
                  y_ref, c_ref, n_ref, m_ref, *, c, dk, dv, valid_len):
    @pl.when(pl.program_id(1) == 0)
    def _():
        c_ref[...] = c0_ref[...]
        n_ref[...] = n0_ref[...]
        m_ref[...] = m0_ref[...]

    gt = gt_ref[0]
    rowc = lax.broadcasted_iota(jnp.int32, (c, LANES), 0)
    lanec = lax.broadcasted_iota(jnp.int32, (c, LANES), 1)
    gt = jnp.where(rowc < valid_len, gt, jnp.where(lanec < D_HEADS, -1e30, 0.0))
    cum = _tri_cumsum(gt, c)
    gt_t = gt.T
    cum_t = cum.T
    r_i = lax.broadcasted_iota(jnp.int32, (c, c), 0)
    c_i = lax.broadcasted_iota(jnp.int32, (c, c), 1)
    tril = c_i <= r_i
    for h in range(D_HEADS):
        b_col = cum[:, D_HEADS + h:D_HEADS + h + 1]
        b_row = cum_t[D_HEADS + h:D_HEADS + h + 1, :]
        li_col = gt[:, h:h + 1]
        li_row = gt_t[h:h + 1, :]
        m_prev = m_ref[0, h][:, 0:1]
        n_row = n_ref[0, h]
        c_st = c_ref[0, h]
        qh = q_ref[0, :, h * dk:(h + 1) * dk]
        kh = k_ref[0, :, h * dk:(h + 1) * dk]
        vh16 = v_ref[0, :, h * dv:(h + 1) * dv].astype(BF16)
        qh16 = qh.astype(BF16)

        dmat = jnp.where(tril, b_col - b_row + li_row, NEG_INF)
        inter = b_col + m_prev
        m_t = jnp.maximum(inter, jnp.max(dmat, axis=-1, keepdims=True))
        w_inter = jnp.exp(inter - m_t)
        wqk = jnp.exp(dmat - m_t) * _dot_nt(qh16, kh.astype(BF16))
        num = w_inter * jnp.dot(qh16, c_st.astype(BF16), preferred_element_type=F32) \
            + jnp.dot(wqk.astype(BF16), vh16, preferred_element_type=F32)
        den = w_inter * jnp.sum(qh * n_row, axis=-1, keepdims=True) + jnp.sum(wqk, axis=-1, keepdims=True)
        hh = num / jnp.maximum(jnp.abs(den), jnp.exp(-m_t))

        b_last = b_col[c - 1:c, :]
        a_col = b_last - b_col + li_col
        m_new = jnp.maximum(b_last + m_prev, jnp.max(a_col, axis=0, keepdims=True))
        w_c = jnp.exp(b_last + m_prev - m_new)
        w_s = jnp.exp(a_col - m_new)
        ks = w_s * kh
        c_ref[0, h] = w_c * c_st + _dot_tn(ks.astype(BF16), vh16)
        n_ref[0, h] = w_c * n_row + jnp.sum(ks, axis=0, keepdims=True)
        m_ref[0, h] = jnp.broadcast_to(m_new, (1, LANES))

        y = hh * lax.rsqrt(jnp.mean(hh * hh, axis=-1, keepdims=True) + EPS) * ng_ref[:, h * dv:(h + 1) * dv]
        y_ref[0, :, h * dv:(h + 1) * dv] = (y * og_ref[0, :, h * dv:(h + 1) * dv]).astype(y_ref.dtype)


def _mlstm_recurrence(q, k, v, og, gates, norm_g, c0, n0, m0, c, valid_len):
    b, seq, _ = q.shape
    dk = q.shape[2] // D_HEADS
    dv = v.shape[2] // D_HEADS
    assert seq % c == 0 and (seq == c or valid_len == c)
    n0 = n0.reshape(b, D_HEADS, 1, dk)
    m0 = jnp.broadcast_to(m0.reshape(b, D_HEADS, 1, 1), (b, D_HEADS, 1, LANES))
    kern = functools.partial(_mlstm_kernel, c=c, dk=dk, dv=dv, valid_len=valid_len)

    def tok(w):
        return pl.BlockSpec((1, c, w), lambda i, j: (i, j, 0))

    c_spec = pl.BlockSpec((1, D_HEADS, dk, dv), lambda i, j: (i, 0, 0, 0))
    n_spec = pl.BlockSpec((1, D_HEADS, 1, dk), lambda i, j: (i, 0, 0, 0))
    m_spec = pl.BlockSpec((1, D_HEADS, 1, LANES), lambda i, j: (i, 0, 0, 0))
    y, c_out, n_out, m_out = pl.pallas_call(
        kern,
        grid=(b, seq // c),
        in_specs=[tok(D_HEADS * dk), tok(D_HEADS * dk), tok(D_HEADS * dv), tok(D_HEADS * dv), tok(LANES),
                  pl.BlockSpec((1, D_HEADS * dv), lambda i, j: (0, 0)), c_spec, n_spec, m_spec],
        out_specs=[tok(D_HEADS * dv), c_spec, n_spec, m_spec],
        out_shape=[jax.ShapeDtypeStruct((b, seq, D_HEADS * dv), BF16),
                   jax.ShapeDtypeStruct((b, D_HEADS, dk, dv), F32),
                   jax.ShapeDtypeStruct((b, D_HEADS, 1, dk), F32),
                   jax.ShapeDtypeStruct((b, D_HEADS, 1, LANES), F32)],
        compiler_params=_cparams("parallel", "arbitrary"),
        name="mlstm_recurrence",
    )(q, k, v, og, gates, norm_g.reshape(1, D_HEADS * dv), c0, n0, m0)
    return y, c_out, n_out.reshape(b, D_HEADS, dk), m_out[:, :, 0, 0]


def _ident(z):
    return (z,)


def _gelu(z):
    return (jax.nn.gelu(z, approximate=True),)


def _mix_gmlp(x, g, j, w_in, ln_g, ln_b, w_s, b_s, w_out):
    b, seq, d = x.shape
    width = w_in.shape[2] // 2
    chunk = w_s.shape[1]
    c = chunk if seq % chunk == 0 else seq
    x2 = x.reshape(b * seq, d)
    (z,) = _norm_matmul(x2, g, w_in, j, 0, 2 * width, _gelu, (F32,))
    a, v = _gmlp_gate(z.reshape(b * seq // c, c, 2 * width), ln_g, ln_b, w_s, b_s, c)
    y = _matmul_residual(a.reshape(b * seq, width), w_out, j, x2)
    return y.reshape(b, seq, d), v.reshape(b, seq, width)


def _moba_qkv(x, g, j, w_qkv):
    b, seq, d = x.shape
    hdm = w_qkv.shape[2] // 3
    x2 = x.reshape(b * seq, d)
    qkv = _norm_matmul_parts(x2, g, w_qkv, j, 0, [(hdm, _ident, (F32,), ())] * 3)
    return [p[0].reshape(b, seq, hdm) for p in qkv]


def _hgrn_project(x2, g, j, w_in, lb):
    hdm = w_in.shape[2] // 4
    lb = lb.reshape(1, hdm)

    def silu(z):
        return (z * _sigmoid(z),)

    def forget(z, lbv):
        logf = jnp.log(lbv + (1.0 - lbv) * _sigmoid(z))
        return logf, (1.0 - lbv) * _sigmoid(-z)

    (q,), (logf, k), (iv,), (gate,) = _norm_matmul_parts(
        x2, g, w_in, j, 0, [(hdm, silu, (F32,), ()), (hdm, forget, (F32, F32), (lb,)),
                            (hdm, _ident, (F32,), ()), (hdm, silu, (F32,), ())])
    return q, k, iv, logf, gate


def _mix_hgrn(x, g, j, w_in, lb, norm_g, w_out, s0):
    b, seq, d = x.shape
    x2 = x.reshape(b * seq, d)
    parts = [p.reshape(b, seq, -1) for p in _hgrn_project(x2, g, j, w_in, lb)]
    if seq % 256 == 0:
        t_step, cs, pad = 256, 64, 0
    else:
        t_step = cs = -(-seq // SUB) * SUB
        pad = t_step - seq
        parts = [jnp.pad(p, ((0, 0), (0, pad), (0, 0))) for p in parts]
    y, s = _hgrn_recurrence(*parts, norm_g, s0, t_step, cs)
    y = y[:, :seq].reshape(b * seq, -1)
    return _matmul_residual(y, w_out, j, x2).reshape(b, seq, d), s


def _mix_mlstm(x, g, j, w_in, w_gates, b_gates, norm_g, w_out, c0, n0, m0):
    b, seq, d = x.shape
    dk = c0.shape[2]
    dv = c0.shape[3]
    nq, nv = D_HEADS * dk, D_HEADS * dv
    x2 = x.reshape(b * seq, d)

    def kscale(z):
        return (z * (dk ** -0.5),)

    def ogate(z):
        return (_sigmoid(z),)

    def gates_fn(z, bias):
        zz = z + bias
        lane = lax.broadcasted_iota(jnp.int32, zz.shape, 1)
        return (jnp.where(lane < D_HEADS, zz, _log_sigmoid(zz)),)

    (q,), (k,), (v,), (og,) = _norm_matmul_parts(
        x2, g, w_in, j, 0, [(nq, _ident, (BF16,), ()), (nq, kscale, (BF16,), ()),
                            (nv, _ident, (BF16,), ()), (nv, ogate, (BF16,), ())])
    (gt,) = _norm_matmul(x2, g, w_gates, 0, 0, LANES, gates_fn, (F32,), extras=(b_gates,))
    parts = [p.reshape(b, seq, -1) for p in (q, k, v, og, gt)]
    if seq % 256 == 0:
        c, valid = 256, 256
    else:
        c = -(-seq // SUB) * SUB
        valid = seq
        parts = [jnp.pad(p, ((0, 0), (0, c - seq), (0, 0))) for p in parts]
    y, c_out, n_out, m_out = _mlstm_recurrence(*parts, norm_g, c0, n0, m0, c, valid)
    y = y[:, :seq].reshape(b * seq, nv)
    return _matmul_residual(y, w_out, j, x2).reshape(b, seq, d), c_out, n_out, m_out


def kernel(x_prompt, x_sample, cache_k, cache_v, page_table, state_hgrn, state_mlstm_c, state_mlstm_n, state_mlstm_m, norm_mix, norm_ffn, norm_final, w_ffn_up, w_ffn_down, a_w_in, a_ln_g, a_ln_b, a_w_s, a_b_s, a_w_out, b_w_qkv, b_w_out, c_w_in, c_lower_bound, c_norm_g, c_w_out, d_w_in, d_b_gates, d_norm_g, d_w_out):
    depth = norm_mix.shape[0]
    bsz, seq, d = x_prompt.shape
    dbsz, dseq, _ = x_sample.shape
    slopes = jnp.asarray(2.0 ** (-8.0 * np.arange(1, B_HEADS + 1) / B_HEADS), F32)
    lbs = jax.nn.softmax(c_lower_bound.astype(F32), axis=0)
    lbs = jnp.cumsum(lbs, axis=0) - lbs[0]
    hdm = b_w_qkv.shape[2] // 3

    xp, xs = x_prompt, x_sample
    outs = {k: [] for k in ("av", "kp", "vp", "ks", "vs", "hp", "hs", "cp", "np", "mp", "cs", "ns", "ms")}
    for layer in range(depth):
        kind = layer % N_MIXERS
        j = layer // N_MIXERS
        g = norm_mix[layer]
        if kind == 0:
            args = (j, a_w_in, a_ln_g[j], a_ln_b[j], a_w_s[j], a_b_s[j], a_w_out)
            xp, _ = _mix_gmlp(xp, g, *args)
            xs, vrow = _mix_gmlp(xs, g, *args)
            outs["av"].append(vrow)
        elif kind == 1:
            qp, kp, vp = _moba_qkv(xp, g, j, b_w_qkv)
            op = _moba_prompt(qp, kp, vp, slopes)
            xp = _matmul_residual(op.reshape(bsz * seq, hdm), b_w_out, j,
                                  xp.reshape(bsz * seq, d)).reshape(bsz, seq, d)
            qs, ks, vs = _moba_qkv(xs, g, j, b_w_qkv)
            os_ = _moba_sample(qs, ks, vs, cache_k, cache_v, j, page_table, slopes)
            xs = _matmul_residual(os_.reshape(dbsz * dseq, hdm), b_w_out, j,
                                  xs.reshape(dbsz * dseq, d)).reshape(dbsz, dseq, d)
            hd = hdm // B_HEADS
            outs["kp"].append(kp.reshape(bsz, seq, B_HEADS, hd))
            outs["vp"].append(vp.reshape(bsz, seq, B_HEADS, hd))
            outs["ks"].append(ks.reshape(dbsz, dseq, B_HEADS, hd))
            outs["vs"].append(vs.reshape(dbsz, dseq, B_HEADS, hd))
        elif kind == 2:
            args = (j, c_w_in, lbs[layer], c_norm_g[j], c_w_out)
            s0 = jnp.zeros((bsz,) + state_hgrn.shape[2:], F32)
            xp, sp = _mix_hgrn(xp, g, *args, s0)
            xs, ss = _mix_hgrn(xs, g, *args, state_hgrn[j])
            outs["hp"].append(sp)
            outs["hs"].append(ss)
        else:
            ng = 2 * D_HEADS * (state_mlstm_c.shape[3] + state_mlstm_c.shape[4])
            w_gates = jnp.pad(d_w_in[j][:, ng:], ((0, 0), (0, LANES - 2 * D_HEADS)))[None]
            b_gates = jnp.pad(d_b_gates[j], (0, LANES - 2 * D_HEADS)).reshape(1, LANES)
            args = (j, d_w_in, w_gates, b_gates, d_norm_g[j], d_w_out)
            zc = jnp.zeros((bsz,) + state_mlstm_c.shape[2:], F32)
            zn = jnp.zeros((bsz,) + state_mlstm_n.shape[2:], F32)
            zm = jnp.zeros((bsz,) + state_mlstm_m.shape[2:], F32)
            xp, cp, np_, mp = _mix_mlstm(xp, g, *args, zc, zn, zm)
            xs, cs, ns, ms = _mix_mlstm(xs, g, *args, state_mlstm_c[j], state_mlstm_n[j], state_mlstm_m[j])
            for key, val in zip(("cp", "np", "mp", "cs", "ns", "ms"), (cp, np_, mp, cs, ns, ms)):
                outs[key].append(val)
        xp = _ffn(xp.reshape(bsz * seq, d), norm_ffn[layer], w_ffn_up, w_ffn_down, layer).reshape(bsz, seq, d)
        xs = _ffn(xs.reshape(dbsz * dseq, d), norm_ffn[layer], w_ffn_up, w_ffn_down, layer).reshape(dbsz, dseq, d)
    y_prompt = _rmsnorm(xp.reshape(bsz * seq, d), norm_final).reshape(bsz, seq, d)
    y_sample = _rmsnorm(xs.reshape(dbsz * dseq, d), norm_final).reshape(dbsz, dseq, d)
    st = {k: jnp.stack(v) for k, v in outs.items()}
    return (y_prompt, y_sample, st["av"], st["kp"], st["vp"], st["ks"], st["vs"], st["hp"], st["hs"],
            st["cp"], st["np"], st["mp"], st["cs"], st["ns"], st["ms"])
```

```python
import functools

import jax
import jax.numpy as jnp
import numpy as np
from jax import lax
from jax.experimental import pallas as pl
from jax.experimental.pallas import tpu as pltpu

F32 = jnp.float32
BF16 = jnp.bfloat16
EPS = 1e-6
NEG_INF = float("-inf")

A_GROUPS = 8
B_HEADS = 16
B_BLOCK = 256
B_TOPK = 3
C_HEADS = 16
C_DK = 128
D_HEADS = 4
N_MIXERS = 4

LANES = 128
SUBLANES = 8
VMEM_LIMIT_BYTES = 56 * 1024 * 1024
SUB = 16
MM_TILE_M = 1024
MM_TILE_N = 512
FFN_TILE_F = 512
RES_TILE_N = 1024


def _cparams(*sem):
    return pltpu.CompilerParams(dimension_semantics=sem, vmem_limit_bytes=VMEM_LIMIT_BYTES)


def _sigmoid(x):
    return 1.0 / (1.0 + jnp.exp(-x))


def _log_sigmoid(x):
    return jnp.minimum(x, 0.0) - jnp.log(1.0 + jnp.exp(-jnp.abs(x)))


def _split3(x):
    hi = x.astype(BF16)
    r1 = x - hi.astype(F32)
    mid = r1.astype(BF16)
    lo = (r1 - mid.astype(F32)).astype(BF16)
    return hi, mid, lo


def _tri_cumsum(x, block):
    n = x.shape[0]
    r = lax.broadcasted_iota(jnp.int32, (n, n), 0)
    c = lax.broadcasted_iota(jnp.int32, (n, n), 1)
    tri = jnp.where((c <= r) & ((r // block) == (c // block)), 1.0, 0.0).astype(BF16)
    hi, mid, lo = _split3(x)
    out = jnp.dot(tri, lo, preferred_element_type=F32)
    out = out + jnp.dot(tri, mid, preferred_element_type=F32)
    return out + jnp.dot(tri, hi, preferred_element_type=F32)


def _dot_nt(a, b):
    return lax.dot_general(a, b, (((1,), (1,)), ((), ())), preferred_element_type=F32)


def _dot_tn(a, b):
    return lax.dot_general(a, b, (((0,), (0,)), ((), ())), preferred_element_type=F32)


def _dot_nt_f32(a, b):
    a0, a1, a2 = _split3(a)
    b0, b1, b2 = _split3(b)
    out = _dot_nt(a1, b1) + _dot_nt(a0, b2) + _dot_nt(a2, b0)
    out = out + _dot_nt(a0, b1) + _dot_nt(a1, b0)
    return out + _dot_nt(a0, b0)


def _norm_mm_kernel(x_ref, g_ref, w_ref, *rest, parts):
    n_extra = sum(p[3] for p in parts)
    n_out = sum(p[4] for p in parts)
    extras, outs, xn_ref = rest[:n_extra], rest[n_extra:n_extra + n_out], rest[n_extra + n_out]
    j = pl.program_id(1)

    @pl.when(j == 0)
    def _():
        x = x_ref[...]
        y = x * lax.rsqrt(jnp.mean(x * x, axis=-1, keepdims=True) + EPS)
        xn_ref[...] = (y * g_ref[...]).astype(BF16)

    z = jnp.dot(xn_ref[...], w_ref[...].astype(BF16), preferred_element_type=F32)
    e0 = o0 = 0
    for jb0, nb, epilogue, ne, no in parts:
        def finish(epilogue=epilogue, ex=extras[e0:e0 + ne], os=outs[o0:o0 + no]):
            for o_ref, r in zip(os, epilogue(z, *[e[...] for e in ex])):
                o_ref[...] = r.astype(o_ref.dtype)

        if len(parts) == 1:
            finish()
        else:
            pl.when((j >= jb0) & (j < jb0 + nb))(finish)
        e0, o0 = e0 + ne, o0 + no


def _norm_matmul_parts(x, g, w, layer, col0, parts):
    m, d = x.shape
    tm = min(m, MM_TILE_M)
    tn = min(min(p[0] for p in parts), MM_TILE_N)
    assert m % tm == 0 and col0 % tn == 0 and all(p[0] % tn == 0 for p in parts)
    jw = col0 // tn
    kparts, extra_specs, out_specs, out_shapes, extra_args = [], [], [], [], []
    jb0 = 0
    for ncols, epilogue, out_dtypes, extras in parts:
        nb = ncols // tn

        def col_block(i, j, jb0=jb0, nb=nb):
            return jnp.clip(j - jb0, 0, nb - 1)

        kparts.append((jb0, nb, epilogue, len(extras), len(out_dtypes)))
        extra_specs += [pl.BlockSpec((1, tn), lambda i, j, cb=col_block: (0, cb(i, j))) for _ in extras]
        out_specs += [pl.BlockSpec((tm, tn), lambda i, j, cb=col_block: (i, cb(i, j))) for _ in out_dtypes]
        out_shapes += [jax.ShapeDtypeStruct((m, ncols), dt) for dt in out_dtypes]
        extra_args += list(extras)
        jb0 += nb
    res = pl.pallas_call(
        functools.partial(_norm_mm_kernel, parts=tuple(kparts)),
        grid=(m // tm, jb0),
        in_specs=[pl.BlockSpec((tm, d), lambda i, j: (i, 0)),
                  pl.BlockSpec((1, d), lambda i, j: (0, 0)),
                  pl.BlockSpec((None, d, tn), lambda i, j: (layer, 0, j + jw))] + extra_specs,
        out_specs=out_specs,
        out_shape=out_shapes,
        scratch_shapes=[pltpu.VMEM((tm, d), BF16)],
        compiler_params=_cparams("parallel", "arbitrary"),
        name="norm_matmul",
    )(x, g.reshape(1, d), w, *extra_args)
    out, o0 = [], 0
    for p in parts:
        out.append(tuple(res[o0:o0 + len(p[2])]))
        o0 += len(p[2])
    return out


def _norm_matmul(x, g, w, layer, col0, ncols, epilogue, out_dtypes, extras=()):
    return _norm_matmul_parts(x, g, w, layer, col0, [(ncols, epilogue, out_dtypes, extras)])[0]


def _mm_res_kernel(a_ref, w_ref, r_ref, o_ref):
    o_ref[...] = r_ref[...] + jnp.dot(a_ref[...].astype(BF16), w_ref[...].astype(BF16),
                                      preferred_element_type=F32)


def _matmul_residual(a, w, layer, res):
    m, k = a.shape
    n = w.shape[2]
    tm = min(m, MM_TILE_M)
    tn = min(n, RES_TILE_N)
    assert m % tm == 0 and n % tn == 0
    return pl.pallas_call(
        _mm_res_kernel,
        grid=(m // tm, n // tn),
        in_specs=[pl.BlockSpec((tm, k), lambda i, j: (i, 0)),
                  pl.BlockSpec((None, k, tn), lambda i, j: (layer, 0, j)),
                  pl.BlockSpec((tm, tn), lambda i, j: (i, j))],
        out_specs=pl.BlockSpec((tm, tn), lambda i, j: (i, j)),
        out_shape=jax.ShapeDtypeStruct((m, n), F32),
        compiler_params=_cparams("parallel", "parallel"),
        name="matmul_residual",
    )(a, w, res)


def _ffn_kernel(x_ref, g_ref, wu_ref, wd_ref, o_ref, xn_ref):
    f = pl.program_id(1)

    @pl.when(f == 0)
    def _():
        x = x_ref[...]
        y = x * lax.rsqrt(jnp.mean(x * x, axis=-1, keepdims=True) + EPS)
        xn_ref[...] = (y * g_ref[...]).astype(BF16)
        o_ref[...] = x

    h = jnp.maximum(jnp.dot(xn_ref[...], wu_ref[...].astype(BF16), preferred_element_type=F32), 0.0)
    o_ref[...] += jnp.dot((h * h).astype(BF16), wd_ref[...].astype(BF16), preferred_element_type=F32)


def _ffn(x, g, w_up, w_down, layer):
    m, d = x.shape
    dff = w_up.shape[2]
    tm = min(m, MM_TILE_M)
    tf = min(dff, FFN_TILE_F)
    assert m % tm == 0 and dff % tf == 0
    return pl.pallas_call(
        _ffn_kernel,
        grid=(m // tm, dff // tf),
        in_specs=[pl.BlockSpec((tm, d), lambda i, f: (i, 0), pipeline_mode=pl.Buffered(1)),
                  pl.BlockSpec((1, d), lambda i, f: (0, 0)),
                  pl.BlockSpec((None, d, tf), lambda i, f: (layer, 0, f)),
                  pl.BlockSpec((None, tf, d), lambda i, f: (layer, f, 0))],
        out_specs=pl.BlockSpec((tm, d), lambda i, f: (i, 0)),
        out_shape=jax.ShapeDtypeStruct((m, d), F32),
        scratch_shapes=[pltpu.VMEM((tm, d), BF16)],
        compiler_params=_cparams("parallel", "arbitrary"),
        name="ffn",
    )(x, g.reshape(1, d), w_up, w_down)


def _rmsnorm_kernel(x_ref, g_ref, o_ref):
    x = x_ref[...]
    o_ref[...] = x * lax.rsqrt(jnp.mean(x * x, axis=-1, keepdims=True) + EPS) * g_ref[...]


def _rmsnorm(x, g):
    m, d = x.shape
    tm = min(m, 512)
    return pl.pallas_call(
        _rmsnorm_kernel,
        grid=(m // tm,),
        in_specs=[pl.BlockSpec((tm, d), lambda i: (i, 0)), pl.BlockSpec((1, d), lambda i: (0, 0))],
        out_specs=pl.BlockSpec((tm, d), lambda i: (i, 0)),
        out_shape=jax.ShapeDtypeStruct((m, d), F32),
        compiler_params=_cparams("parallel"),
        name="final_rmsnorm",
    )(x, g.reshape(1, d))


def _gmlp_gate_kernel(z_ref, lg_ref, lb_ref, ws_ref, bs_ref, a_ref, v_ref, *, c, width):
    gw = width // A_GROUPS
    z = z_ref[0]
    u = z[:, :width]
    vr = z[:, width:]
    mu = jnp.mean(vr, axis=-1, keepdims=True)
    vc = vr - mu
    v = vc * lax.rsqrt(jnp.mean(vc * vc, axis=-1, keepdims=True) + EPS) * lg_ref[...] + lb_ref[...]
    v_ref[0] = v
    row = lax.broadcasted_iota(jnp.int32, (c, c), 0)
    col = lax.broadcasted_iota(jnp.int32, (c, c), 1)
    bs = bs_ref[...]
    for g in range(A_GROUPS):
        ws = jnp.where(col <= row, ws_ref[g], 0.0)
        vg = v[:, g * gw:(g + 1) * gw]
        if c >= 2 * SUBLANES:
            s = jnp.dot(ws.astype(BF16), vg.astype(BF16), preferred_element_type=F32)
        else:
            s = jnp.zeros((c, gw), F32)
            for t in range(c):
                s = s + ws[:, t:t + 1] * vg[t:t + 1, :]
        s = s + bs[:, g:g + 1]
        a_ref[0, :, g * gw:(g + 1) * gw] = (u[:, g * gw:(g + 1) * gw] * s).astype(a_ref.dtype)


def _gmlp_gate(z, ln_g, ln_b, w_s, b_s, c):
    nb, _, w2 = z.shape
    width = w2 // 2
    ws = w_s[:, :c, :c]
    bs_t = b_s[:, :c].T
    kern = functools.partial(_gmlp_gate_kernel, c=c, width=width)
    return pl.pallas_call(
        kern,
        grid=(nb,),
        in_specs=[pl.BlockSpec((1, c, w2), lambda i: (i, 0, 0)),
                  pl.BlockSpec((1, width), lambda i: (0, 0)),
                  pl.BlockSpec((1, width), lambda i: (0, 0)),
                  pl.BlockSpec((A_GROUPS, c, c), lambda i: (0, 0, 0)),
                  pl.BlockSpec((c, A_GROUPS), lambda i: (0, 0))],
        out_specs=[pl.BlockSpec((1, c, width), lambda i: (i, 0, 0)),
                   pl.BlockSpec((1, c, width), lambda i: (i, 0, 0))],
        out_shape=[jax.ShapeDtypeStruct((nb, c, width), BF16 if c % (2 * SUBLANES) == 0 else F32),
                   jax.ShapeDtypeStruct((nb, c, width), F32)],
        compiler_params=_cparams("parallel"),
        name="gmlp_gate",
    )(z, ln_g.reshape(1, width), ln_b.reshape(1, width), ws, bs_t)


MASKED = -1e30
V_PAD_ROWS = 16


MOBA_HEADS_PER_STEP = 2


def _moba_prompt_kernel(slope_ref, q_ref, k_ref, v_ref, o_ref, kaug_ref, vt_ref, means_ref, bias_ref,
                        *, nblk, hd, hps):
    blk = B_BLOCK
    scale = hd ** -0.5
    nbp = means_ref.shape[1]
    heads = range(hps)
    slopes = [slope_ref[pl.program_id(1) * hps + hh] for hh in heads]

    lane_blk = lax.broadcasted_iota(jnp.int32, (blk, hd), 1)
    ones_rows = jnp.where(lax.broadcasted_iota(jnp.int32, (V_PAD_ROWS, blk), 0) == 0, 1.0, 0.0).astype(BF16)
    key_i = lax.broadcasted_iota(jnp.int32, (blk, blk), 0)
    qry_i = lax.broadcasted_iota(jnp.int32, (blk, blk), 1)
    dist = (qry_i - key_i).astype(F32)
    means_ref[...] = jnp.zeros_like(means_ref)
    for hh in heads:
        cols = slice(hh * hd, (hh + 1) * hd)
        for n in range(nblk):
            kb = k_ref[0, n * blk:(n + 1) * blk, cols]
            means_ref[hh, n:n + 1, :] = jnp.sum(kb, axis=0, keepdims=True) * (1.0 / blk)
            kaug_ref[hh, n * blk:(n + 1) * blk, 0:hd] = kb.astype(BF16)
            kaug_ref[hh, n * blk:(n + 1) * blk, hd:2 * hd] = jnp.where(lane_blk == n, 1.0, 0.0).astype(BF16)
            vt_ref[hh, n, 0:hd, :] = v_ref[0, n * blk:(n + 1) * blk, cols].T.astype(BF16)
            vt_ref[hh, n, hd:hd + V_PAD_ROWS, :] = ones_rows
        bias_ref[hh, 0:blk, :] = -slopes[hh] * (dist + blk)
        bias_ref[hh, blk:2 * blk, :] = -slopes[hh] * dist
        bias_ref[hh, 2 * blk:3 * blk, :] = -slopes[hh] * (dist + blk)
        bias_ref[hh, 3 * blk:4 * blk, :] = jnp.where(dist >= 0, -slopes[hh] * dist, NEG_INF)
    mean_parts = [_split3(means_ref[hh])[:2] for hh in heads]
    rown = lax.broadcasted_iota(jnp.int32, (nbp, blk), 0)

    def update(st, s, c, vt):
        m, acc = st
        m_new = jnp.maximum(m, jnp.max(s, axis=0, keepdims=True) + c)
        p = jnp.exp(s - (m_new - c))
        return m_new, jnp.exp(m - m_new) * acc + jnp.dot(vt, p.astype(BF16), preferred_element_type=F32)

    def q_block(qb, carry):
        q0 = pl.multiple_of(qb * blk, blk)
        qaugs = []
        for hh in heads:
            q = q_ref[0, pl.ds(q0, blk), hh * hd:(hh + 1) * hd]
            q_hi, q_mid, _ = _split3(q)
            mean_hi, mean_mid = mean_parts[hh]
            sc = _dot_nt(mean_hi, q_mid) + _dot_nt(mean_mid, q_hi) + _dot_nt(mean_hi, q_hi)
            sc = jnp.where(rown < qb, sc, NEG_INF)
            cnt = jnp.zeros((nbp, blk), F32)
            for n2 in range(nblk):
                rn = sc[n2:n2 + 1, :]
                ahead = (rn > sc) | ((rn == sc) & (rown > n2))
                cnt = cnt + jnp.where(ahead, 1.0, 0.0)
            keep = ((cnt < B_TOPK) & (rown < qb)) | (rown == qb)
            selb = jnp.where(keep, 0.0, MASKED)
            qaug = jnp.concatenate([(q * scale).T, selb, jnp.zeros((hd - nbp, blk), F32)], axis=0).astype(BF16)
            qaugs.append(qaug)

        def first_tile(s, vt):
            m = jnp.max(s, axis=0, keepdims=True)
            return m, jnp.dot(vt, jnp.exp(s - m).astype(BF16), preferred_element_type=F32)

        def own_only():
            return tuple(first_tile(
                jnp.dot(kaug_ref[hh, pl.ds(q0, blk), :], qaugs[hh], preferred_element_type=F32)
                + bias_ref[hh, 3 * blk:4 * blk, :], vt_ref[hh, qb]) for hh in heads)

        def own_and_previous():
            k0 = pl.multiple_of(q0 - blk, blk)
            return tuple(first_tile(
                jnp.dot(kaug_ref[hh, pl.ds(k0, 2 * blk), :], qaugs[hh], preferred_element_type=F32)
                + bias_ref[hh, 2 * blk:4 * blk, :],
                jnp.concatenate([vt_ref[hh, qb - 1], vt_ref[hh, qb]], axis=1)) for hh in heads)

        def pair_scores(hh, n):
            k0 = pl.multiple_of(jnp.minimum(n, nblk - 2) * blk, blk)
            return jnp.dot(kaug_ref[hh, pl.ds(k0, 2 * blk), :], qaugs[hh], preferred_element_type=F32)

        def two_blocks(i, carry):
            st, scores = carry
            n = 2 * i
            nxt = tuple(pair_scores(hh, n + 2) for hh in heads)
            out = []
            for hh in heads:
                c = -slopes[hh] * lax.convert_element_type((qb - n - 1) * blk, F32)
                s = scores[hh] + bias_ref[hh, 0:2 * blk, :]
                out.append(update(st[hh], s, c, jnp.concatenate([vt_ref[hh, n], vt_ref[hh, n + 1]], axis=1)))
            return tuple(out), nxt

        odd = qb % 2
        state = lax.cond(odd == 1, own_and_previous, own_only)
        first = tuple(pair_scores(hh, 0) for hh in heads)
        state, _ = lax.fori_loop(0, (qb - odd) // 2, two_blocks, (state, first))
        for hh in heads:
            acc = state[hh][1]
            o_t = acc[0:hd, :] / acc[hd:hd + 1, :]
            o_ref[0, pl.ds(q0, blk), hh * hd:(hh + 1) * hd] = o_t.T.astype(o_ref.dtype)
        return carry

    lax.fori_loop(0, nblk, q_block, 0)


def _moba_prompt(q, k, v, slopes):
    b, seq, hdm = q.shape
    hd = hdm // B_HEADS
    hps = MOBA_HEADS_PER_STEP
    assert seq % B_BLOCK == 0 and hd == LANES and B_HEADS % hps == 0
    nblk = seq // B_BLOCK
    nbp = -(-nblk // SUBLANES) * SUBLANES
    assert nbp <= hd and nblk >= 2
    kern = functools.partial(_moba_prompt_kernel, nblk=nblk, hd=hd, hps=hps)
    spec = pl.BlockSpec((1, seq, hps * hd), lambda i, h: (i, 0, h))
    return pl.pallas_call(
        kern,
        grid=(b, B_HEADS // hps),
        in_specs=[pl.BlockSpec(memory_space=pltpu.SMEM), spec, spec, spec],
        out_specs=spec,
        out_shape=jax.ShapeDtypeStruct((b, seq, hdm), BF16),
        scratch_shapes=[pltpu.VMEM((hps, seq, 2 * hd), BF16),
                        pltpu.VMEM((hps, nblk, hd + V_PAD_ROWS, B_BLOCK), BF16),
                        pltpu.VMEM((hps, nbp, hd), F32),
                        pltpu.VMEM((hps, 4 * B_BLOCK, B_BLOCK), F32)],
        compiler_params=_cparams("parallel", "parallel"),
        name="moba_prompt",
    )(slopes, q, k, v)


MEAN_PAGES_PER_STEP = 8


def _page_mean_kernel(pt_ref, *refs, ppb, pps):
    o_ref = refs[pps]
    for blk in range(pps // ppb):
        s = jnp.sum(refs[blk * ppb][0, 0], axis=0)
        for pg in range(1, ppb):
            s = s + jnp.sum(refs[blk * ppb + pg][0, 0], axis=0)
        o_ref[0, blk] = s * (1.0 / B_BLOCK)


def _cache_block_means(cache, layer, page_table):
    _, npool, page, heads, hd = cache.shape
    db, npg = page_table.shape
    ppb = B_BLOCK // page
    nblk = npg // ppb
    pps = MEAN_PAGES_PER_STEP
    while npg % pps:
        pps //= 2
    assert pps % ppb == 0
    kern = functools.partial(_page_mean_kernel, ppb=ppb, pps=pps)

    def page_spec(r):
        return pl.BlockSpec((1, 1, page, heads, hd), lambda i, p, pt: (layer, pt[i, p * pps + r], 0, 0, 0))

    return pl.pallas_call(
        kern,
        grid_spec=pltpu.PrefetchScalarGridSpec(
            num_scalar_prefetch=1,
            grid=(db, npg // pps),
            in_specs=[page_spec(r) for r in range(pps)],
            out_specs=pl.BlockSpec((1, pps // ppb, heads, hd), lambda i, p, pt: (i, p, 0, 0)),
        ),
        out_shape=jax.ShapeDtypeStruct((db, nblk, heads, hd), F32),
        compiler_params=_cparams("parallel", "parallel"),
        name="moba_cache_means",
    )(page_table, *([cache] * pps))


def _moba_select_kernel(q_ref, mean_ref, o_ref, *, s_len, hd, n_valid):
    rows, hdm = q_ref.shape[1], q_ref.shape[2]
    nblk = mean_ref.shape[1]
    r_i = lax.broadcasted_iota(jnp.int32, (rows, hdm), 0)
    c_i = lax.broadcasted_iota(jnp.int32, (rows, hdm), 1)
    qx = jnp.where((c_i // hd) == (r_i // s_len), q_ref[0], 0.0)
    sc = _dot_nt_f32(qx, mean_ref[0])
    lane = lax.broadcasted_iota(jnp.int32, (rows, nblk), 1).astype(F32)
    out_lane = lax.broadcasted_iota(jnp.int32, (rows, LANES), 1)
    out = jnp.zeros((rows, LANES), F32)
    for t in range(n_valid):
        mx = jnp.max(sc, axis=-1, keepdims=True)
        idx = jnp.min(jnp.where(sc == mx, lane, float(nblk)), axis=-1, keepdims=True)
        out = jnp.where(out_lane == t, idx, out)
        sc = jnp.where(lane == idx, NEG_INF, sc)
    o_ref[0] = out.astype(jnp.int32)


def _moba_select(q_rep, means, s_len, n_valid):
    db, rows, hdm = q_rep.shape
    nblk = means.shape[1]
    kern = functools.partial(_moba_select_kernel, s_len=s_len, hd=hdm // B_HEADS, n_valid=n_valid)
    return pl.pallas_call(
        kern,
        grid=(db,),
        in_specs=[pl.BlockSpec((1, rows, hdm), lambda i: (i, 0, 0)),
                  pl.BlockSpec((1, nblk, hdm), lambda i: (i, 0, 0))],
        out_specs=pl.BlockSpec((1, rows, LANES), lambda i: (i, 0, 0)),
        out_shape=jax.ShapeDtypeStruct((db, rows, LANES), jnp.int32),
        compiler_params=_cparams("parallel"),
        name="moba_sample_select",
    )(q_rep, means)


def _moba_sample_kernel(phys_ref, idx_ref, slope_ref, q_ref, kn_ref, vn_ref, ck_hbm, cv_hbm, o_ref,
                        kbuf, vbuf, sems, q8_ref, *, layer, s_len, hd, past, n_valid, nsel, ppb):
    npages = nsel * ppb
    g = pl.program_id(0)
    h = g % B_HEADS
    slot = g % 2
    slope = slope_ref[h]
    scale = hd ** -0.5
    page = kbuf.shape[2]
    row8 = lax.broadcasted_iota(jnp.int32, (SUBLANES, 1), 0)

    def page_copies(step, to_slot):
        head = step % B_HEADS
        out = []
        for r in range(npages):
            pool_row = phys_ref[step * npages + r]
            out.append(pltpu.make_async_copy(ck_hbm.at[layer, pool_row, :, head, :], kbuf.at[to_slot, r],
                                             sems.at[0, to_slot]))
            out.append(pltpu.make_async_copy(cv_hbm.at[layer, pool_row, :, head, :], vbuf.at[to_slot, r],
                                             sems.at[1, to_slot]))
        return out

    @pl.when(g == 0)
    def _():
        for cp in page_copies(g, slot):
            cp.start()

    @pl.when(g + 1 < pl.num_programs(0))
    def _():
        for cp in page_copies(g + 1, 1 - slot):
            cp.start()

    for cp in page_copies(g, slot):
        cp.wait()
    k_refs = [kbuf.at[slot, r] for r in range(npages)]
    v_refs = [vbuf.at[slot, r] for r in range(npages)]

    q8_ref[...] = jnp.zeros_like(q8_ref)
    q8_ref[0:s_len, :] = q_ref[0] * scale
    q8 = q8_ref[...]
    kn = kn_ref[0]
    vn = vn_ref[0]
    s_cols = []
    for c in range(s_len):
        sc = jnp.sum(q8 * kn[c:c + 1, :], axis=-1, keepdims=True) - slope * (row8 - c).astype(F32)
        s_cols.append(jnp.where(row8 >= c, sc, NEG_INF))
    m = s_cols[0]
    for c in range(1, s_len):
        m = jnp.maximum(m, s_cols[c])

    qs = q8.astype(BF16)
    col = lax.broadcasted_iota(jnp.int32, (SUBLANES, ppb * page), 1)
    row = lax.broadcasted_iota(jnp.int32, (SUBLANES, ppb * page), 0)
    live = [j for j in range(nsel) if j % B_TOPK < n_valid]
    tiles = []
    for j in live:
        blk_id = idx_ref[g * nsel + j]
        kb = jnp.concatenate([k_refs[j * ppb + pg][...] for pg in range(ppb)], axis=0).astype(BF16)
        dist = (past + row - blk_id * B_BLOCK - col).astype(F32)
        s = jnp.where(row == j // B_TOPK, _dot_nt(qs, kb) - slope * dist, NEG_INF)
        tiles.append(s)
        m = jnp.maximum(m, jnp.max(s, axis=-1, keepdims=True))

    l = jnp.zeros((SUBLANES, 1), F32)
    acc = jnp.zeros((SUBLANES, hd), F32)
    for c in range(s_len):
        p = jnp.exp(s_cols[c] - m)
        l = l + p
        acc = acc + p * vn[c:c + 1, :]
    for j, s in zip(live, tiles):
        p = jnp.exp(s - m)
        vb = jnp.concatenate([v_refs[j * ppb + pg][...] for pg in range(ppb)], axis=0).astype(BF16)
        l = l + jnp.sum(p, axis=-1, keepdims=True)
        acc = acc + jnp.dot(p.astype(BF16), vb, preferred_element_type=F32)
    o_ref[0] = (acc / l)[0:s_len, :]


def _moba_sample(q, k_new, v_new, cache_k, cache_v, layer, page_table, slopes):
    db, s_len, hdm = q.shape
    hd = hdm // B_HEADS
    page = cache_k.shape[2]
    npg = page_table.shape[1]
    past = npg * page
    assert B_BLOCK % page == 0 and past % B_BLOCK == 0 and s_len <= SUBLANES
    own_blk = past // B_BLOCK
    assert own_blk >= 1
    n_valid = min(own_blk, B_TOPK)
    nsel = s_len * B_TOPK

    means = _cache_block_means(cache_k, layer, page_table).reshape(db, own_blk, hdm)
    q_rep = jnp.tile(q, (1, B_HEADS, 1))
    top = _moba_select(q_rep, means, s_len, n_valid)[:, :, :B_TOPK]
    top = jnp.minimum(top, own_blk - 1)
    idx = top.reshape(db, B_HEADS, nsel)
    ppb = B_BLOCK // page
    pages = idx[..., None] * ppb + jnp.arange(ppb, dtype=jnp.int32)
    phys = jnp.take_along_axis(page_table[:, None, :], pages.reshape(db, 1, -1), axis=2)
    phys = phys.reshape(-1).astype(jnp.int32)
    idx_flat = idx.reshape(-1).astype(jnp.int32)

    npages = nsel * ppb
    row_spec = pl.BlockSpec((1, s_len, hd), lambda g, ph, ix: (g // B_HEADS, 0, g % B_HEADS))
    kern = functools.partial(_moba_sample_kernel, layer=layer, s_len=s_len, hd=hd, past=past, n_valid=n_valid,
                             nsel=nsel, ppb=ppb)
    return pl.pallas_call(
        kern,
        grid_spec=pltpu.PrefetchScalarGridSpec(
            num_scalar_prefetch=2,
            grid=(db * B_HEADS,),
            in_specs=[pl.BlockSpec(memory_space=pltpu.SMEM), row_spec, row_spec, row_spec,
                      pl.BlockSpec(memory_space=pl.ANY), pl.BlockSpec(memory_space=pl.ANY)],
            out_specs=row_spec,
            scratch_shapes=[pltpu.VMEM((2, npages, page, hd), F32), pltpu.VMEM((2, npages, page, hd), F32),
                            pltpu.SemaphoreType.DMA((2, 2)), pltpu.VMEM((SUBLANES, hd), F32)],
        ),
        out_shape=jax.ShapeDtypeStruct((db, s_len, hdm), F32),
        compiler_params=_cparams("arbitrary"),
        name="moba_sample_attend",
    )(phys, idx_flat, slopes, q, k_new, v_new, cache_k, cache_v)


def _hgrn_kernel(q_ref, k_ref, i_ref, f_ref, g_ref, ng_ref, s0_ref, y_ref, s_ref, st_ref, gc_ref, o_ref,
                 *, t_step, cs, hps):
    c = pl.program_id(2)
    nsub = cs // SUB
    dk = dv = LANES

    @pl.when(c == 0)
    def _():
        for hh in range(hps):
            st_ref[hh] = s0_ref[0, hh].T

    gc_ref[...] = _tri_cumsum(f_ref[0], cs)
    rsub = lax.broadcasted_iota(jnp.int32, (SUB, 1), 0)

    def chunk(ci, carry):
        r0 = pl.multiple_of(ci * cs, cs)
        for hh in range(hps):
            q = q_ref[0, pl.ds(r0, cs), hh * dk:(hh + 1) * dk]
            k = k_ref[0, pl.ds(r0, cs), hh * dk:(hh + 1) * dk]
            iv = i_ref[0, pl.ds(r0, cs), hh * dv:(hh + 1) * dv]
            gcum = gc_ref[pl.ds(r0, cs), hh * dk:(hh + 1) * dk]
            st = st_ref[hh]
            st16 = st.astype(BF16)
            iv16 = iv.astype(BF16)
            o_inter = _dot_nt((q * jnp.exp(gcum)).astype(BF16), st16)
            for si in range(nsub):
                lo, hi = si * SUB, (si + 1) * SUB
                g_i = gcum[lo:hi]
                q_i = q[lo:hi]
                k_i = k[lo:hi]
                i_i = iv[lo:hi]
                acc = o_inter[lo:hi]
                if si > 0:
                    gref = gcum[lo - 1:lo]
                    qt = (q_i * jnp.exp(g_i - gref)).astype(BF16)
                    kt = (k[:lo] * jnp.exp(gref - gcum[:lo])).astype(BF16)
                    att = _dot_nt(qt, kt)
                    acc = acc + jnp.dot(att.astype(BF16), iv16[:lo], preferred_element_type=F32)
                for s in range(SUB):
                    e = jnp.exp(jnp.where(rsub >= s, g_i - g_i[s:s + 1], NEG_INF))
                    a = jnp.sum(q_i * k_i[s:s + 1] * e, axis=-1, keepdims=True)
                    acc = acc + a * i_i[s:s + 1]
                o_ref[pl.ds(r0 + lo, SUB), hh * dv:(hh + 1) * dv] = acc
            glast = gcum[cs - 1:cs]
            kh = (k * jnp.exp(glast - gcum)).astype(BF16)
            st_ref[hh] = st * jnp.exp(glast) + _dot_tn(iv16, kh)
        return carry

    lax.fori_loop(0, t_step // cs, chunk, 0)
    for hh in range(hps):
        o = o_ref[:, hh * dv:(hh + 1) * dv]
        y = o * lax.rsqrt(jnp.mean(o * o, axis=-1, keepdims=True) + EPS) * ng_ref[...]
        y_ref[0, :, hh * dv:(hh + 1) * dv] = (y * g_ref[0, :, hh * dv:(hh + 1) * dv]).astype(y_ref.dtype)

    @pl.when(c == pl.num_programs(2) - 1)
    def _():
        for hh in range(hps):
            s_ref[0, hh] = st_ref[hh].T


HGRN_HEADS_PER_STEP = 16


def _hgrn_recurrence(q, k, iv, logf, gate, norm_g, s0, t_step, cs):
    b, seq, hdm = q.shape
    dk = C_DK
    dv = hdm // C_HEADS
    hps = HGRN_HEADS_PER_STEP
    assert dk == LANES and dv == LANES and seq % t_step == 0 and t_step % cs == 0 and cs % SUB == 0
    assert C_HEADS % hps == 0
    kern = functools.partial(_hgrn_kernel, t_step=t_step, cs=cs, hps=hps)
    tok = pl.BlockSpec((1, t_step, hps * dv), lambda i, h, c: (i, c, h))
    st = pl.BlockSpec((1, hps, dk, dv), lambda i, h, c: (i, h, 0, 0))
    return pl.pallas_call(
        kern,
        grid=(b, C_HEADS // hps, seq // t_step),
        in_specs=[tok, tok, tok, tok, tok, pl.BlockSpec((1, dv), lambda i, h, c: (0, 0)), st],
        out_specs=[tok, st],
        out_shape=[jax.ShapeDtypeStruct((b, seq, hdm), BF16),
                   jax.ShapeDtypeStruct((b, C_HEADS, dk, dv), F32)],
        scratch_shapes=[pltpu.VMEM((hps, dv, dk), F32), pltpu.VMEM((t_step, hps * dk), F32),
                        pltpu.VMEM((t_step, hps * dv), F32)],
        compiler_params=_cparams("parallel", "parallel", "arbitrary"),
        name="hgrn_recurrence",
    )(q, k, iv, logf, gate, norm_g.reshape(1, dv), s0)


def _mlstm_kernel(q_ref, k_ref, v_ref, og_ref, gt_ref, ng_ref, c0_ref, n0_ref, m0_ref,
                  y_ref, c_ref, n_ref, m_ref, *, c, dk, dv, valid_len):
    @pl.when(pl.program_id(1) == 0)
    def _():
        c_ref[...] = c0_ref[...]
        n_ref[...] = n0_ref[...]
        m_ref[...] = m0_ref[...]

    gt = gt_ref[0]
    rowc = lax.broadcasted_iota(jnp.int32, (c, LANES), 0)
    lanec = lax.broadcasted_iota(jnp.int32, (c, LANES), 1)
    gt = jnp.where(rowc < valid_len, gt, jnp.where(lanec < D_HEADS, -1e30, 0.0))
    cum = _tri_cumsum(gt, c)
    gt_t = gt.T
    cum_t = cum.T
    r_i = lax.broadcasted_iota(jnp.int32, (c, c), 0)
    c_i = lax.broadcasted_iota(jnp.int32, (c, c), 1)
    tril = c_i <= r_i
    for h in range(D_HEADS):
        b_col = cum[:, D_HEADS + h:D_HEADS + h + 1]
        b_row = cum_t[D_HEADS + h:D_HEADS + h + 1, :]
        li_col = gt[:, h:h + 1]
        li_row = gt_t[h:h + 1, :]
        m_prev = m_ref[0, h][:, 0:1]
        n_row = n_ref[0, h]
        c_st = c_ref[0, h]
        qh = q_ref[0, :, h * dk:(h + 1) * dk]
        kh = k_ref[0, :, h * dk:(h + 1) * dk]
        vh16 = v_ref[0, :, h * dv:(h + 1) * dv].astype(BF16)
        qh16 = qh.astype(BF16)

        dmat = jnp.where(tril, b_col - b_row + li_row, NEG_INF)
        inter = b_col + m_prev
        m_t = jnp.maximum(inter, jnp.max(dmat, axis=-1, keepdims=True))
        w_inter = jnp.exp(inter - m_t)
        wqk = jnp.exp(dmat - m_t) * _dot_nt(qh16, kh.astype(BF16))
        num = w_inter * jnp.dot(qh16, c_st.astype(BF16), preferred_element_type=F32) \
            + jnp.dot(wqk.astype(BF16), vh16, preferred_element_type=F32)
        den = w_inter * jnp.sum(qh * n_row, axis=-1, keepdims=True) + jnp.sum(wqk, axis=-1, keepdims=True)
        hh = num / jnp.maximum(jnp.abs(den), jnp.exp(-m_t))

        b_last = b_col[c - 1:c, :]
        a_col = b_last - b_col + li_col
        m_new = jnp.maximum(b_last + m_prev, jnp.max(a_col, axis=0, keepdims=True))
        w_c = jnp.exp(b_last + m_prev - m_new)
        w_s = jnp.exp(a_col - m_new)
        ks = w_s * kh
        c_ref[0, h] = w_c * c_st + _dot_tn(ks.astype(BF16), vh16)
        n_ref[0, h] = w_c * n_row + jnp.sum(ks, axis=0, keepdims=True)
        m_ref[0, h] = jnp.broadcast_to(m_new, (1, LANES))

        y = hh * lax.rsqrt(jnp.mean(hh * hh, axis=-1, keepdims=True) + EPS) * ng_ref[:, h * dv:(h + 1) * dv]
        y_ref[0, :, h * dv:(h + 1) * dv] = (y * og_ref[0, :, h * dv:(h + 1) * dv]).astype(y_ref.dtype)


def _mlstm_recurrence(q, k, v, og, gates, norm_g, c0, n0, m0, c, valid_len):
    b, seq, _ = q.shape
    dk = q.shape[2] // D_HEADS
    dv = v.shape[2] // D_HEADS
    assert seq % c == 0 and (seq == c or valid_len == c)
    n0 = n0.reshape(b, D_HEADS, 1, dk)
    m0 = jnp.broadcast_to(m0.reshape(b, D_HEADS, 1, 1), (b, D_HEADS, 1, LANES))
    kern = functools.partial(_mlstm_kernel, c=c, dk=dk, dv=dv, valid_len=valid_len)

    def tok(w):
        return pl.BlockSpec((1, c, w), lambda i, j: (i, j, 0))

    c_spec = pl.BlockSpec((1, D_HEADS, dk, dv), lambda i, j: (i, 0, 0, 0))
    n_spec = pl.BlockSpec((1, D_HEADS, 1, dk), lambda i, j: (i, 0, 0, 0))
    m_spec = pl.BlockSpec((1, D_HEADS, 1, LANES), lambda i, j: (i, 0, 0, 0))
    y, c_out, n_out, m_out = pl.pallas_call(
        kern,
        grid=(b, seq // c),
        in_specs=[tok(D_HEADS * dk), tok(D_HEADS * dk), tok(D_HEADS * dv), tok(D_HEADS * dv), tok(LANES),
                  pl.BlockSpec((1, D_HEADS * dv), lambda i, j: (0, 0)), c_spec, n_spec, m_spec],
        out_specs=[tok(D_HEADS * dv), c_spec, n_spec, m_spec],
        out_shape=[jax.ShapeDtypeStruct((b, seq, D_HEADS * dv), BF16),
                   jax.ShapeDtypeStruct((b, D_HEADS, dk, dv), F32),
                   jax.ShapeDtypeStruct((b, D_HEADS, 1, dk), F32),
                   jax.ShapeDtypeStruct((b, D_HEADS, 1, LANES), F32)],
        compiler_params=_cparams("parallel", "arbitrary"),
        name="mlstm_recurrence",
    )(q, k, v, og, gates, norm_g.reshape(1, D_HEADS * dv), c0, n0, m0)
    return y, c_out, n_out.reshape(b, D_HEADS, dk), m_out[:, :, 0, 0]


def _ident(z):
    return (z,)


def _gelu(z):
    return (jax.nn.gelu(z, approximate=True),)


def _mix_gmlp(x, g, j, w_in, ln_g, ln_b, w_s, b_s, w_out):
    b, seq, d = x.shape
    width = w_in.shape[2] // 2
    chunk = w_s.shape[1]
    c = chunk if seq % chunk == 0 else seq
    x2 = x.reshape(b * seq, d)
    (z,) = _norm_matmul(x2, g, w_in, j, 0, 2 * width, _gelu, (F32,))
    a, v = _gmlp_gate(z.reshape(b * seq // c, c, 2 * width), ln_g, ln_b, w_s, b_s, c)
    y = _matmul_residual(a.reshape(b * seq, width), w_out, j, x2)
    return y.reshape(b, seq, d), v.reshape(b, seq, width)


def _moba_qkv(x, g, j, w_qkv):
    b, seq, d = x.shape
    hdm = w_qkv.shape[2] // 3
    x2 = x.reshape(b * seq, d)
    qkv = _norm_matmul_parts(x2, g, w_qkv, j, 0, [(hdm, _ident, (F32,), ())] * 3)
    return [p[0].reshape(b, seq, hdm) for p in qkv]


def _hgrn_project(x2, g, j, w_in, lb):
    hdm = w_in.shape[2] // 4
    lb = lb.reshape(1, hdm)

    def silu(z):
        return (z * _sigmoid(z),)

    def forget(z, lbv):
        logf = jnp.log(lbv + (1.0 - lbv) * _sigmoid(z))
        return logf, (1.0 - lbv) * _sigmoid(-z)

    (q,), (logf, k), (iv,), (gate,) = _norm_matmul_parts(
        x2, g, w_in, j, 0, [(hdm, silu, (F32,), ()), (hdm, forget, (F32, F32), (lb,)),
                            (hdm, _ident, (F32,), ()), (hdm, silu, (F32,), ())])
    return q, k, iv, logf, gate


def _mix_hgrn(x, g, j, w_in, lb, norm_g, w_out, s0):
    b, seq, d = x.shape
    x2 = x.reshape(b * seq, d)
    parts = [p.reshape(b, seq, -1) for p in _hgrn_project(x2, g, j, w_in, lb)]
    if seq % 256 == 0:
        t_step, cs, pad = 256, 64, 0
    else:
        t_step = cs = -(-seq // SUB) * SUB
        pad = t_step - seq
        parts = [jnp.pad(p, ((0, 0), (0, pad), (0, 0))) for p in parts]
    y, s = _hgrn_recurrence(*parts, norm_g, s0, t_step, cs)
    y = y[:, :seq].reshape(b * seq, -1)
    return _matmul_residual(y, w_out, j, x2).reshape(b, seq, d), s


def _mix_mlstm(x, g, j, w_in, w_gates, b_gates, norm_g, w_out, c0, n0, m0):
    b, seq, d = x.shape
    dk = c0.shape[2]
    dv = c0.shape[3]
    nq, nv = D_HEADS * dk, D_HEADS * dv
    x2 = x.reshape(b * seq, d)

    def kscale(z):
        return (z * (dk ** -0.5),)

    def ogate(z):
        return (_sigmoid(z),)

    def gates_fn(z, bias):
        zz = z + bias
        lane = lax.broadcasted_iota(jnp.int32, zz.shape, 1)
        return (jnp.where(lane < D_HEADS, zz, _log_sigmoid(zz)),)

    (q,), (k,), (v,), (og,) = _norm_matmul_parts(
        x2, g, w_in, j, 0, [(nq, _ident, (BF16,), ()), (nq, kscale, (BF16,), ()),
                            (nv, _ident, (BF16,), ()), (nv, ogate, (BF16,), ())])
    (gt,) = _norm_matmul(x2, g, w_gates, 0, 0, LANES, gates_fn, (F32,), extras=(b_gates,))
    parts = [p.reshape(b, seq, -1) for p in (q, k, v, og, gt)]
    if seq % 256 == 0:
        c, valid = 256, 256
    else:
        c = -(-seq // SUB) * SUB
        valid = seq
        parts = [jnp.pad(p, ((0, 0), (0, c - seq), (0, 0))) for p in parts]
    y, c_out, n_out, m_out = _mlstm_recurrence(*parts, norm_g, c0, n0, m0, c, valid)
    y = y[:, :seq].reshape(b * seq, nv)
    return _matmul_residual(y, w_out, j, x2).reshape(b, seq, d), c_out, n_out, m_out


def kernel(x_prompt, x_sample, cache_k, cache_v, page_table, state_hgrn, state_mlstm_c, state_mlstm_n, state_mlstm_m, norm_mix, norm_ffn, norm_final, w_ffn_up, w_ffn_down, a_w_in, a_ln_g, a_ln_b, a_w_s, a_b_s, a_w_out, b_w_qkv, b_w_out, c_w_in, c_lower_bound, c_norm_g, c_w_out, d_w_in, d_b_gates, d_norm_g, d_w_out):
    depth = norm_mix.shape[0]
    bsz, seq, d = x_prompt.shape
    dbsz, dseq, _ = x_sample.shape
    slopes = jnp.asarray(2.0 ** (-8.0 * np.arange(1, B_HEADS + 1) / B_HEADS), F32)
    lbs = jax.nn.softmax(c_lower_bound.astype(F32), axis=0)
    lbs = jnp.cumsum(lbs, axis=0) - lbs[0]
    hdm = b_w_qkv.shape[2] // 3

    xp, xs = x_prompt, x_sample
    outs = {k: [] for k in ("av", "kp", "vp", "ks", "vs", "hp", "hs", "cp", "np", "mp", "cs", "ns", "ms")}
    for layer in range(depth):
        kind = layer % N_MIXERS
        j = layer // N_MIXERS
        g = norm_mix[layer]
        if kind == 0:
            args = (j, a_w_in, a_ln_g[j], a_ln_b[j], a_w_s[j], a_b_s[j], a_w_out)
            xp, _ = _mix_gmlp(xp, g, *args)
            xs, vrow = _mix_gmlp(xs, g, *args)
            outs["av"].append(vrow)
        elif kind == 1:
            qp, kp, vp = _moba_qkv(xp, g, j, b_w_qkv)
            op = _moba_prompt(qp, kp, vp, slopes)
            xp = _matmul_residual(op.reshape(bsz * seq, hdm), b_w_out, j,
                                  xp.reshape(bsz * seq, d)).reshape(bsz, seq, d)
            qs, ks, vs = _moba_qkv(xs, g, j, b_w_qkv)
            os_ = _moba_sample(qs, ks, vs, cache_k, cache_v, j, page_table, slopes)
            xs = _matmul_residual(os_.reshape(dbsz * dseq, hdm), b_w_out, j,
                                  xs.reshape(dbsz * dseq, d)).reshape(dbsz, dseq, d)
            hd = hdm // B_HEADS
            outs["kp"].append(kp.reshape(bsz, seq, B_HEADS, hd))
            outs["vp"].append(vp.reshape(bsz, seq, B_HEADS, hd))
            outs["ks"].append(ks.reshape(dbsz, dseq, B_HEADS, hd))
            outs["vs"].append(vs.reshape(dbsz, dseq, B_HEADS, hd))
        elif kind == 2:
            args = (j, c_w_in, lbs[layer], c_norm_g[j], c_w_out)
            s0 = jnp.zeros((bsz,) + state_hgrn.shape[2:], F32)
            xp, sp = _mix_hgrn(xp, g, *args, s0)
            xs, ss = _mix_hgrn(xs, g, *args, state_hgrn[j])
            outs["hp"].append(sp)
            outs["hs"].append(ss)
        else:
            ng = 2 * D_HEADS * (state_mlstm_c.shape[3] + state_mlstm_c.shape[4])
            w_gates = jnp.pad(d_w_in[j][:, ng:], ((0, 0), (0, LANES - 2 * D_HEADS)))[None]
            b_gates = jnp.pad(d_b_gates[j], (0, LANES - 2 * D_HEADS)).reshape(1, LANES)
            args = (j, d_w_in, w_gates, b_gates, d_norm_g[j], d_w_out)
            zc = jnp.zeros((bsz,) + state_mlstm_c.shape[2:], F32)
            zn = jnp.zeros((bsz,) + state_mlstm_n.shape[2:], F32)
            zm = jnp.zeros((bsz,) + state_mlstm_m.shape[2:], F32)
            xp, cp, np_, mp = _mix_mlstm(xp, g, *args, zc, zn, zm)
            xs, cs, ns, ms = _mix_mlstm(xs, g, *args, state_mlstm_c[j], state_mlstm_n[j], state_mlstm_m[j])
            for key, val in zip(("cp", "np", "mp", "cs", "ns", "ms"), (cp, np_, mp, cs, ns, ms)):
                outs[key].append(val)
        xp = _ffn(xp.reshape(bsz * seq, d), norm_ffn[layer], w_ffn_up, w_ffn_down, layer).reshape(bsz, seq, d)
        xs = _ffn(xs.reshape(dbsz * dseq, d), norm_ffn[layer], w_ffn_up, w_ffn_down, layer).reshape(dbsz, dseq, d)
    y_prompt = _rmsnorm(xp.reshape(bsz * seq, d), norm_final).reshape(bsz, seq, d)
    y_sample = _rmsnorm(xs.reshape(dbsz * dseq, d), norm_final).reshape(dbsz, dseq, d)
    st = {k: jnp.stack(v) for k, v in outs.items()}
    return (y_prompt, y_sample, st["av"], st["kp"], st["vp"], st["ks"], st["vs"], st["hp"], st["hs"],
            st["cp"], st["np"], st["mp"], st["cs"], st["ns"], st["ms"])
```

```python
import functools

import jax
import jax.numpy as jnp
import numpy as np
from jax import lax
from jax.experimental import pallas as pl
from jax.experimental.pallas import tpu as pltpu

F32 = jnp.float32
BF16 = jnp.bfloat16
EPS = 1e-6
NEG_INF = float("-inf")

A_GROUPS = 8
B_HEADS = 16
B_BLOCK = 256
B_TOPK = 3
C_HEADS = 16
C_DK = 128
D_HEADS = 4
N_MIXERS = 4

LANES = 128
SUBLANES = 8
VMEM_LIMIT_BYTES = 56 * 1024 * 1024
SUB = 16
SEQ_PAD = 16
MM_TILE_M = 1024
MM_TILE_N = 512
FFN_TILE_F = 512
RES_TILE_N = 1024


def _cparams(*sem):
    return pltpu.CompilerParams(dimension_semantics=sem, vmem_limit_bytes=VMEM_LIMIT_BYTES)


def _sigmoid(x):
    return 1.0 / (1.0 + jnp.exp(-x))


def _log_sigmoid(x):
    return jnp.minimum(x, 0.0) - jnp.log(1.0 + jnp.exp(-jnp.abs(x)))


def _split3(x):
    hi = x.astype(BF16)
    r1 = x - hi.astype(F32)
    mid = r1.astype(BF16)
    lo = (r1 - mid.astype(F32)).astype(BF16)
    return hi, mid, lo


def _tri_cumsum(x, block):
    n = x.shape[0]
    r = lax.broadcasted_iota(jnp.int32, (n, n), 0)
    c = lax.broadcasted_iota(jnp.int32, (n, n), 1)
    tri = jnp.where((c <= r) & ((r // block) == (c // block)), 1.0, 0.0).astype(BF16)
    hi, mid, lo = _split3(x)
    out = jnp.dot(tri, lo, preferred_element_type=F32)
    out = out + jnp.dot(tri, mid, preferred_element_type=F32)
    return out + jnp.dot(tri, hi, preferred_element_type=F32)


def _dot_nt(a, b):
    return lax.dot_general(a, b, (((1,), (1,)), ((), ())), preferred_element_type=F32)


def _dot_tn(a, b):
    return lax.dot_general(a, b, (((0,), (0,)), ((), ())), preferred_element_type=F32)


def _dot_nt_f32(a, b):
    a0, a1, a2 = _split3(a)
    b0, b1, b2 = _split3(b)
    out = _dot_nt(a1, b1) + _dot_nt(a0, b2) + _dot_nt(a2, b0)
    out = out + _dot_nt(a0, b1) + _dot_nt(a1, b0)
    return out + _dot_nt(a0, b0)


def _norm_mm_kernel(x_ref, g_ref, w_ref, *rest, parts):
    n_extra = sum(p[3] for p in parts)
    n_out = sum(p[4] for p in parts)
    extras, outs, xn_ref = rest[:n_extra], rest[n_extra:n_extra + n_out], rest[n_extra + n_out]
    j = pl.program_id(1)

    @pl.when(j == 0)
    def _():
        x = x_ref[...]
        y = x * lax.rsqrt(jnp.mean(x * x, axis=-1, keepdims=True) + EPS)
        xn_ref[...] = (y * g_ref[...]).astype(BF16)

    z = jnp.dot(xn_ref[...], w_ref[...].astype(BF16), preferred_element_type=F32)
    e0 = o0 = 0
    for jb0, nb, epilogue, ne, no in parts:
        def finish(epilogue=epilogue, ex=extras[e0:e0 + ne], os=outs[o0:o0 + no]):
            for o_ref, r in zip(os, epilogue(z, *[e[...] for e in ex])):
                o_ref[...] = r.astype(o_ref.dtype)

        if len(parts) == 1:
            finish()
        else:
            pl.when((j >= jb0) & (j < jb0 + nb))(finish)
        e0, o0 = e0 + ne, o0 + no


def _norm_matmul_parts(x, g, w, layer, col0, parts):
    m, d = x.shape
    tm = min(m, MM_TILE_M)
    tn = min(min(p[0] for p in parts), MM_TILE_N)
    assert m % tm == 0 and col0 % tn == 0 and all(p[0] % tn == 0 for p in parts)
    jw = col0 // tn
    kparts, extra_specs, out_specs, out_shapes, extra_args = [], [], [], [], []
    jb0 = 0
    for ncols, epilogue, out_dtypes, extras in parts:
        nb = ncols // tn

        def col_block(i, j, jb0=jb0, nb=nb):
            return jnp.clip(j - jb0, 0, nb - 1)

        kparts.append((jb0, nb, epilogue, len(extras), len(out_dtypes)))
        extra_specs += [pl.BlockSpec((1, tn), lambda i, j, cb=col_block: (0, cb(i, j))) for _ in extras]
        out_specs += [pl.BlockSpec((tm, tn), lambda i, j, cb=col_block: (i, cb(i, j))) for _ in out_dtypes]
        out_shapes += [jax.ShapeDtypeStruct((m, ncols), dt) for dt in out_dtypes]
        extra_args += list(extras)
        jb0 += nb
    res = pl.pallas_call(
        functools.partial(_norm_mm_kernel, parts=tuple(kparts)),
        grid=(m // tm, jb0),
        in_specs=[pl.BlockSpec((tm, d), lambda i, j: (i, 0)),
                  pl.BlockSpec((1, d), lambda i, j: (0, 0)),
                  pl.BlockSpec((None, d, tn), lambda i, j: (layer, 0, j + jw))] + extra_specs,
        out_specs=out_specs,
        out_shape=out_shapes,
        scratch_shapes=[pltpu.VMEM((tm, d), BF16)],
        compiler_params=_cparams("parallel", "arbitrary"),
        name="norm_matmul",
    )(x, g.reshape(1, d), w, *extra_args)
    out, o0 = [], 0
    for p in parts:
        out.append(tuple(res[o0:o0 + len(p[2])]))
        o0 += len(p[2])
    return out


def _norm_matmul(x, g, w, layer, col0, ncols, epilogue, out_dtypes, extras=()):
    return _norm_matmul_parts(x, g, w, layer, col0, [(ncols, epilogue, out_dtypes, extras)])[0]


def _mm_res_kernel(a_ref, w_ref, r_ref, o_ref):
    o_ref[...] = r_ref[...] + jnp.dot(a_ref[...].astype(BF16), w_ref[...].astype(BF16),
                                      preferred_element_type=F32)


def _matmul_residual(a, w, layer, res):
    m, k = a.shape
    n = w.shape[2]
    tm = min(m, MM_TILE_M)
    tn = min(n, RES_TILE_N)
    assert m % tm == 0 and n % tn == 0
    return pl.pallas_call(
        _mm_res_kernel,
        grid=(m // tm, n // tn),
        in_specs=[pl.BlockSpec((tm, k), lambda i, j: (i, 0)),
                  pl.BlockSpec((None, k, tn), lambda i, j: (layer, 0, j)),
                  pl.BlockSpec((tm, tn), lambda i, j: (i, j))],
        out_specs=pl.BlockSpec((tm, tn), lambda i, j: (i, j)),
        out_shape=jax.ShapeDtypeStruct((m, n), F32),
        compiler_params=_cparams("parallel", "parallel"),
        name="matmul_residual",
    )(a, w, res)


def _ffn_kernel(x_ref, g_ref, wu_ref, wd_ref, o_ref, xn_ref):
    f = pl.program_id(1)

    @pl.when(f == 0)
    def _():
        x = x_ref[...]
        y = x * lax.rsqrt(jnp.mean(x * x, axis=-1, keepdims=True) + EPS)
        xn_ref[...] = (y * g_ref[...]).astype(BF16)
        o_ref[...] = x

    h = jnp.maximum(jnp.dot(xn_ref[...], wu_ref[...].astype(BF16), preferred_element_type=F32), 0.0)
    o_ref[...] += jnp.dot((h * h).astype(BF16), wd_ref[...].astype(BF16), preferred_element_type=F32)


def _ffn(x, g, w_up, w_down, layer):
    m, d = x.shape
    dff = w_up.shape[2]
    tm = min(m, MM_TILE_M)
    tf = min(dff, FFN_TILE_F)
    assert m % tm == 0 and dff % tf == 0
    return pl.pallas_call(
        _ffn_kernel,
        grid=(m // tm, dff // tf),
        in_specs=[pl.BlockSpec((tm, d), lambda i, f: (i, 0), pipeline_mode=pl.Buffered(1)),
                  pl.BlockSpec((1, d), lambda i, f: (0, 0)),
                  pl.BlockSpec((None, d, tf), lambda i, f: (layer, 0, f)),
                  pl.BlockSpec((None, tf, d), lambda i, f: (layer, f, 0))],
        out_specs=pl.BlockSpec((tm, d), lambda i, f: (i, 0)),
        out_shape=jax.ShapeDtypeStruct((m, d), F32),
        scratch_shapes=[pltpu.VMEM((tm, d), BF16)],
        compiler_params=_cparams("parallel", "arbitrary"),
        name="ffn",
    )(x, g.reshape(1, d), w_up, w_down)


def _rmsnorm_kernel(x_ref, g_ref, o_ref):
    x = x_ref[...]
    o_ref[...] = x * lax.rsqrt(jnp.mean(x * x, axis=-1, keepdims=True) + EPS) * g_ref[...]


def _rmsnorm(x, g):
    m, d = x.shape
    tm = min(m, 512)
    return pl.pallas_call(
        _rmsnorm_kernel,
        grid=(m // tm,),
        in_specs=[pl.BlockSpec((tm, d), lambda i: (i, 0)), pl.BlockSpec((1, d), lambda i: (0, 0))],
        out_specs=pl.BlockSpec((tm, d), lambda i: (i, 0)),
        out_shape=jax.ShapeDtypeStruct((m, d), F32),
        compiler_params=_cparams("parallel"),
        name="final_rmsnorm",
    )(x, g.reshape(1, d))


def _gmlp_gate_kernel(z_ref, lg_ref, lb_ref, ws_ref, bs_ref, a_ref, v_ref, *, c, width):
    gw = width // A_GROUPS
    z = z_ref[0]
    u = z[:, :width]
    vr = z[:, width:]
    mu = jnp.mean(vr, axis=-1, keepdims=True)
    vc = vr - mu
    v = vc * lax.rsqrt(jnp.mean(vc * vc, axis=-1, keepdims=True) + EPS) * lg_ref[...] + lb_ref[...]
    v_ref[0] = v
    row = lax.broadcasted_iota(jnp.int32, (c, c), 0)
    col = lax.broadcasted_iota(jnp.int32, (c, c), 1)
    bs = bs_ref[...]
    for g in range(A_GROUPS):
        ws = jnp.where(col <= row, ws_ref[g], 0.0)
        vg = v[:, g * gw:(g + 1) * gw]
        if c >= 2 * SUBLANES:
            s = jnp.dot(ws.astype(BF16), vg.astype(BF16), preferred_element_type=F32)
        else:
            s = jnp.zeros((c, gw), F32)
            for t in range(c):
                s = s + ws[:, t:t + 1] * vg[t:t + 1, :]
        s = s + bs[:, g:g + 1]
        a_ref[0, :, g * gw:(g + 1) * gw] = (u[:, g * gw:(g + 1) * gw] * s).astype(a_ref.dtype)


def _gmlp_gate(z, ln_g, ln_b, w_s, b_s, c):
    nb, _, w2 = z.shape
    width = w2 // 2
    ws = w_s[:, :c, :c]
    bs_t = b_s[:, :c].T
    kern = functools.partial(_gmlp_gate_kernel, c=c, width=width)
    return pl.pallas_call(
        kern,
        grid=(nb,),
        in_specs=[pl.BlockSpec((1, c, w2), lambda i: (i, 0, 0)),
                  pl.BlockSpec((1, width), lambda i: (0, 0)),
                  pl.BlockSpec((1, width), lambda i: (0, 0)),
                  pl.BlockSpec((A_GROUPS, c, c), lambda i: (0, 0, 0)),
                  pl.BlockSpec((c, A_GROUPS), lambda i: (0, 0))],
        out_specs=[pl.BlockSpec((1, c, width), lambda i: (i, 0, 0)),
                   pl.BlockSpec((1, c, width), lambda i: (i, 0, 0))],
        out_shape=[jax.ShapeDtypeStruct((nb, c, width), BF16 if c % (2 * SUBLANES) == 0 else F32),
                   jax.ShapeDtypeStruct((nb, c, width), F32)],
        compiler_params=_cparams("parallel"),
        name="gmlp_gate",
    )(z, ln_g.reshape(1, width), ln_b.reshape(1, width), ws, bs_t)


MASKED = -1e30
V_PAD_ROWS = 16


MOBA_HEADS_PER_STEP = 2


def _moba_prompt_kernel(slope_ref, q_ref, k_ref, v_ref, o_ref, kaug_ref, vt_ref, means_ref, mask_ref,
                        *, nblk, hd, hps):
    blk = B_BLOCK
    scale = hd ** -0.5
    nbp = means_ref.shape[1]
    heads = range(hps)
    slopes = [slope_ref[pl.program_id(1) * hps + hh] for hh in heads]

    ali = nbp + SUBLANES
    lane_blk = lax.broadcasted_iota(jnp.int32, (blk, hd), 1)
    key_off = lax.broadcasted_iota(jnp.int32, (blk, hd), 0).astype(F32)
    ones_rows = jnp.where(lax.broadcasted_iota(jnp.int32, (V_PAD_ROWS, blk), 0) == 0, 1.0, 0.0).astype(BF16)
    key_i = lax.broadcasted_iota(jnp.int32, (blk, blk), 0)
    qry_i = lax.broadcasted_iota(jnp.int32, (blk, blk), 1)
    mask_ref[0:blk, :] = jnp.zeros((blk, blk), F32)
    mask_ref[blk:2 * blk, :] = jnp.where(qry_i >= key_i, 0.0, NEG_INF)
    row8 = lax.broadcasted_iota(jnp.int32, (SUBLANES, blk), 0)
    means_ref[...] = jnp.zeros_like(means_ref)
    slope_rows = []
    for hh in heads:
        cols = slice(hh * hd, (hh + 1) * hd)
        kaug_ref[hh, 0:blk, 0:hd] = jnp.zeros((blk, hd), BF16)
        kaug_ref[hh, 0:blk, hd:2 * hd] = jnp.where(lane_blk == nbp, 1.0, 0.0).astype(BF16)
        vt_ref[hh, 0] = jnp.zeros((hd + V_PAD_ROWS, blk), BF16)
        for n in range(nblk):
            kb = k_ref[0, n * blk:(n + 1) * blk, cols]
            means_ref[hh, n:n + 1, :] = jnp.sum(kb, axis=0, keepdims=True) * (1.0 / blk)
            kaug_ref[hh, (n + 1) * blk:(n + 2) * blk, 0:hd] = kb.astype(BF16)
            extra = jnp.where(lane_blk == n, 1.0, 0.0)
            extra = jnp.where((lane_blk == ali) | (lane_blk == ali + 1), key_off, extra)
            extra = jnp.where((lane_blk == ali + 2) | (lane_blk == ali + 3), float(n), extra)
            kaug_ref[hh, (n + 1) * blk:(n + 2) * blk, hd:2 * hd] = extra.astype(BF16)
            vt_ref[hh, n + 1, 0:hd, :] = v_ref[0, n * blk:(n + 1) * blk, cols].T.astype(BF16)
            vt_ref[hh, n + 1, hd:hd + V_PAD_ROWS, :] = ones_rows
        s_full = jnp.full((SUBLANES, blk), slopes[hh], F32)
        s_hi = s_full.astype(BF16).astype(F32)
        s_lo = s_full - s_hi
        rows = jnp.where(row8 == 0, s_hi, jnp.where(row8 == 1, s_lo, 0.0))
        rows = jnp.where(row8 == 2, blk * s_hi, jnp.where(row8 == 3, blk * s_lo, rows))
        slope_rows.append(rows)
    mean_parts = [_split3(means_ref[hh])[:2] for hh in heads]
    rown = lax.broadcasted_iota(jnp.int32, (nbp, blk), 0)

    def update(st, s, vt):
        m, acc = st
        m_new = jnp.maximum(m, jnp.max(s, axis=0, keepdims=True))
        p = jnp.exp(s - m_new)
        return m_new, jnp.exp(m - m_new) * acc + jnp.dot(vt, p.astype(BF16), preferred_element_type=F32)

    def query_side(qb):
        q0 = pl.multiple_of(qb * blk, blk)
        qaugs = []
        for hh in heads:
            q = q_ref[0, pl.ds(q0, blk), hh * hd:(hh + 1) * hd]
            q_hi, q_mid, _ = _split3(q)
            mean_hi, mean_mid = mean_parts[hh]
            sc = _dot_nt(mean_hi, q_mid) + _dot_nt(mean_mid, q_hi) + _dot_nt(mean_hi, q_hi)
            sc = jnp.where(rown < qb, sc, NEG_INF)
            cnt = jnp.zeros((nbp, blk), F32)
            for n2 in range(nblk):
                rn = sc[n2:n2 + 1, :]
                ahead = (rn > sc) | ((rn == sc) & (rown > n2))
                cnt = cnt + jnp.where(ahead, 1.0, 0.0)
            keep = ((cnt < B_TOPK) & (rown < qb)) | (rown == qb)
            selb = jnp.where(keep, 0.0, MASKED)
            rest = jnp.zeros((hd - nbp - 2 * SUBLANES, blk), F32)
            qaug = jnp.concatenate([(q * scale).T, selb, jnp.full((SUBLANES, blk), MASKED, F32),
                                    slope_rows[hh], rest], axis=0)
            qaugs.append(qaug.astype(BF16))
        return tuple(qaugs)

    def q_block(qb, qaugs):
        q0 = pl.multiple_of(qb * blk, blk)
        qaugs_next = query_side(jnp.minimum(qb + 1, nblk - 1))

        def pair_scores(hh, slot):
            k0 = pl.multiple_of(jnp.maximum(slot, 0) * blk, blk)
            return jnp.dot(kaug_ref[hh, pl.ds(k0, 2 * blk), :], qaugs[hh], preferred_element_type=F32)

        def pair_values(hh, slot):
            return jnp.concatenate([vt_ref[hh, slot], vt_ref[hh, slot + 1]], axis=1)

        first = tuple(pair_scores(hh, qb - 2) for hh in heads)
        state = []
        for hh in heads:
            s = pair_scores(hh, qb) + mask_ref[...]
            m = jnp.max(s, axis=0, keepdims=True)
            acc = jnp.dot(pair_values(hh, qb), jnp.exp(s - m).astype(BF16), preferred_element_type=F32)
            state.append((m, acc))

        def two_blocks(i, carry):
            st, scores = carry
            slot = qb - 2 - 2 * i
            nxt = tuple(pair_scores(hh, slot - 2) for hh in heads)
            return tuple(update(st[hh], scores[hh], pair_values(hh, slot)) for hh in heads), nxt

        state, _ = lax.fori_loop(0, qb // 2, two_blocks, (tuple(state), first))
        for hh in heads:
            acc = state[hh][1]
            o_t = acc[0:hd, :] / acc[hd:hd + 1, :]
            o_ref[0, pl.ds(q0, blk), hh * hd:(hh + 1) * hd] = o_t.T.astype(o_ref.dtype)
        return qaugs_next

    lax.fori_loop(0, nblk, q_block, query_side(jnp.int32(0)))


def _moba_prompt(q, k, v, slopes):
    b, seq, hdm = q.shape
    hd = hdm // B_HEADS
    hps = MOBA_HEADS_PER_STEP
    assert seq % B_BLOCK == 0 and hd == LANES and B_HEADS % hps == 0
    nblk = seq // B_BLOCK
    nbp = -(-nblk // SUBLANES) * SUBLANES
    assert nbp + 2 * SUBLANES <= hd
    kern = functools.partial(_moba_prompt_kernel, nblk=nblk, hd=hd, hps=hps)
    spec = pl.BlockSpec((1, seq, hps * hd), lambda i, h: (i, 0, h))
    return pl.pallas_call(
        kern,
        grid=(b, B_HEADS // hps),
        in_specs=[pl.BlockSpec(memory_space=pltpu.SMEM), spec, spec, spec],
        out_specs=spec,
        out_shape=jax.ShapeDtypeStruct((b, seq, hdm), BF16),
        scratch_shapes=[pltpu.VMEM((hps, seq + B_BLOCK, 2 * hd), BF16),
                        pltpu.VMEM((hps, nblk + 1, hd + V_PAD_ROWS, B_BLOCK), BF16),
                        pltpu.VMEM((hps, nbp, hd), F32),
                        pltpu.VMEM((2 * B_BLOCK, B_BLOCK), F32)],
        compiler_params=_cparams("parallel", "parallel"),
        name="moba_prompt",
    )(slopes, q, k, v)


MEAN_PAGES_PER_STEP = 8


def _page_mean_kernel(pt_ref, *refs, ppb, pps):
    o_ref = refs[pps]
    for blk in range(pps // ppb):
        s = jnp.sum(refs[blk * ppb][0, 0], axis=0)
        for pg in range(1, ppb):
            s = s + jnp.sum(refs[blk * ppb + pg][0, 0], axis=0)
        o_ref[0, blk] = s * (1.0 / B_BLOCK)


def _cache_block_means(cache, layer, page_table):
    _, npool, page, heads, hd = cache.shape
    db, npg = page_table.shape
    ppb = B_BLOCK // page
    nblk = npg // ppb
    pps = MEAN_PAGES_PER_STEP
    while npg % pps:
        pps //= 2
    assert pps % ppb == 0
    kern = functools.partial(_page_mean_kernel, ppb=ppb, pps=pps)

    def page_spec(r):
        return pl.BlockSpec((1, 1, page, heads, hd), lambda i, p, pt: (layer, pt[i, p * pps + r], 0, 0, 0))

    return pl.pallas_call(
        kern,
        grid_spec=pltpu.PrefetchScalarGridSpec(
            num_scalar_prefetch=1,
            grid=(db, npg // pps),
            in_specs=[page_spec(r) for r in range(pps)],
            out_specs=pl.BlockSpec((1, pps // ppb, heads, hd), lambda i, p, pt: (i, p, 0, 0)),
        ),
        out_shape=jax.ShapeDtypeStruct((db, nblk, heads, hd), F32),
        compiler_params=_cparams("parallel", "parallel"),
        name="moba_cache_means",
    )(page_table, *([cache] * pps))


def _moba_select_kernel(q_ref, mean_ref, o_ref, *, s_len, hd, n_valid):
    rows, hdm = q_ref.shape[1], q_ref.shape[2]
    nblk = mean_ref.shape[1]
    r_i = lax.broadcasted_iota(jnp.int32, (rows, hdm), 0)
    c_i = lax.broadcasted_iota(jnp.int32, (rows, hdm), 1)
    qx = jnp.where((c_i // hd) == (r_i // s_len), q_ref[0], 0.0)
    sc = _dot_nt_f32(qx, mean_ref[0])
    lane = lax.broadcasted_iota(jnp.int32, (rows, nblk), 1).astype(F32)
    out_lane = lax.broadcasted_iota(jnp.int32, (rows, LANES), 1)
    out = jnp.zeros((rows, LANES), F32)
    for t in range(n_valid):
        mx = jnp.max(sc, axis=-1, keepdims=True)
        idx = jnp.min(jnp.where(sc == mx, lane, float(nblk)), axis=-1, keepdims=True)
        out = jnp.where(out_lane == t, idx, out)
        sc = jnp.where(lane == idx, NEG_INF, sc)
    o_ref[0] = out.astype(jnp.int32)


def _moba_select(q_rep, means, s_len, n_valid):
    db, rows, hdm = q_rep.shape
    nblk = means.shape[1]
    kern = functools.partial(_moba_select_kernel, s_len=s_len, hd=hdm // B_HEADS, n_valid=n_valid)
    return pl.pallas_call(
        kern,
        grid=(db,),
        in_specs=[pl.BlockSpec((1, rows, hdm), lambda i: (i, 0, 0)),
                  pl.BlockSpec((1, nblk, hdm), lambda i: (i, 0, 0))],
        out_specs=pl.BlockSpec((1, rows, LANES), lambda i: (i, 0, 0)),
        out_shape=jax.ShapeDtypeStruct((db, rows, LANES), jnp.int32),
        compiler_params=_cparams("parallel"),
        name="moba_sample_select",
    )(q_rep, means)


def _moba_sample_kernel(phys_ref, idx_ref, slope_ref, q_ref, kn_ref, vn_ref, ck_hbm, cv_hbm, o_ref,
                        kbuf, vbuf, sems, q8_ref, *, layer, s_len, hd, past, n_valid, nsel, ppb):
    npages = nsel * ppb
    g = pl.program_id(0)
    h = g % B_HEADS
    slot = g % 2
    slope = slope_ref[h]
    scale = hd ** -0.5
    page = kbuf.shape[2]
    row8 = lax.broadcasted_iota(jnp.int32, (SUBLANES, 1), 0)

    def page_copies(step, to_slot):
        head = step % B_HEADS
        out = []
        for r in range(npages):
            pool_row = phys_ref[step * npages + r]
            out.append(pltpu.make_async_copy(ck_hbm.at[layer, pool_row, :, head, :], kbuf.at[to_slot, r],
                                             sems.at[0, to_slot]))
            out.append(pltpu.make_async_copy(cv_hbm.at[layer, pool_row, :, head, :], vbuf.at[to_slot, r],
                                             sems.at[1, to_slot]))
        return out

    @pl.when(g == 0)
    def _():
        for cp in page_copies(g, slot):
            cp.start()

    @pl.when(g + 1 < pl.num_programs(0))
    def _():
        for cp in page_copies(g + 1, 1 - slot):
            cp.start()

    for cp in page_copies(g, slot):
        cp.wait()
    k_refs = [kbuf.at[slot, r] for r in range(npages)]
    v_refs = [vbuf.at[slot, r] for r in range(npages)]

    q8_ref[...] = jnp.zeros_like(q8_ref)
    q8_ref[0:s_len, :] = q_ref[0] * scale
    q8 = q8_ref[...]
    kn = kn_ref[0]
    vn = vn_ref[0]
    s_cols = []
    for c in range(s_len):
        sc = jnp.sum(q8 * kn[c:c + 1, :], axis=-1, keepdims=True) - slope * (row8 - c).astype(F32)
        s_cols.append(jnp.where(row8 >= c, sc, NEG_INF))
    m = s_cols[0]
    for c in range(1, s_len):
        m = jnp.maximum(m, s_cols[c])

    qs = q8.astype(BF16)
    col = lax.broadcasted_iota(jnp.int32, (SUBLANES, ppb * page), 1)
    row = lax.broadcasted_iota(jnp.int32, (SUBLANES, ppb * page), 0)
    live = [j for j in range(nsel) if j % B_TOPK < n_valid]
    tiles = []
    for j in live:
        blk_id = idx_ref[g * nsel + j]
        kb = jnp.concatenate([k_refs[j * ppb + pg][...] for pg in range(ppb)], axis=0).astype(BF16)
        dist = (past + row - blk_id * B_BLOCK - col).astype(F32)
        s = jnp.where(row == j // B_TOPK, _dot_nt(qs, kb) - slope * dist, NEG_INF)
        tiles.append(s)
        m = jnp.maximum(m, jnp.max(s, axis=-1, keepdims=True))

    l = jnp.zeros((SUBLANES, 1), F32)
    acc = jnp.zeros((SUBLANES, hd), F32)
    for c in range(s_len):
        p = jnp.exp(s_cols[c] - m)
        l = l + p
        acc = acc + p * vn[c:c + 1, :]
    for j, s in zip(live, tiles):
        p = jnp.exp(s - m)
        vb = jnp.concatenate([v_refs[j * ppb + pg][...] for pg in range(ppb)], axis=0).astype(BF16)
        l = l + jnp.sum(p, axis=-1, keepdims=True)
        acc = acc + jnp.dot(p.astype(BF16), vb, preferred_element_type=F32)
    o_ref[0] = (acc / l)[0:s_len, :]


def _moba_sample(q, k_new, v_new, cache_k, cache_v, layer, page_table, slopes):
    db, s_len, hdm = q.shape
    hd = hdm // B_HEADS
    page = cache_k.shape[2]
    npg = page_table.shape[1]
    past = npg * page
    assert B_BLOCK % page == 0 and past % B_BLOCK == 0 and s_len <= SUBLANES
    own_blk = past // B_BLOCK
    assert own_blk >= 1
    n_valid = min(own_blk, B_TOPK)
    nsel = s_len * B_TOPK

    means = _cache_block_means(cache_k, layer, page_table).reshape(db, own_blk, hdm)
    q_rep = jnp.tile(q, (1, B_HEADS, 1))
    top = _moba_select(q_rep, means, s_len, n_valid)[:, :, :B_TOPK]
    top = jnp.minimum(top, own_blk - 1)
    idx = top.reshape(db, B_HEADS, nsel)
    ppb = B_BLOCK // page
    pages = idx[..., None] * ppb + jnp.arange(ppb, dtype=jnp.int32)
    phys = jnp.take_along_axis(page_table[:, None, :], pages.reshape(db, 1, -1), axis=2)
    phys = phys.reshape(-1).astype(jnp.int32)
    idx_flat = idx.reshape(-1).astype(jnp.int32)

    npages = nsel * ppb
    row_spec = pl.BlockSpec((1, s_len, hd), lambda g, ph, ix: (g // B_HEADS, 0, g % B_HEADS))
    kern = functools.partial(_moba_sample_kernel, layer=layer, s_len=s_len, hd=hd, past=past, n_valid=n_valid,
                             nsel=nsel, ppb=ppb)
    return pl.pallas_call(
        kern,
        grid_spec=pltpu.PrefetchScalarGridSpec(
            num_scalar_prefetch=2,
            grid=(db * B_HEADS,),
            in_specs=[pl.BlockSpec(memory_space=pltpu.SMEM), row_spec, row_spec, row_spec,
                      pl.BlockSpec(memory_space=pl.ANY), pl.BlockSpec(memory_space=pl.ANY)],
            out_specs=row_spec,
            scratch_shapes=[pltpu.VMEM((2, npages, page, hd), F32), pltpu.VMEM((2, npages, page, hd), F32),
                            pltpu.SemaphoreType.DMA((2, 2)), pltpu.VMEM((SUBLANES, hd), F32)],
        ),
        out_shape=jax.ShapeDtypeStruct((db, s_len, hdm), F32),
        compiler_params=_cparams("arbitrary"),
        name="moba_sample_attend",
    )(phys, idx_flat, slopes, q, k_new, v_new, cache_k, cache_v)


def _hgrn_kernel(q_ref, k_ref, i_ref, f_ref, g_ref, ng_ref, s0_ref, y_ref, s_ref, st_ref, gc_ref, o_ref,
                 *, t_step, cs, hps):
    c = pl.program_id(2)
    nsub = cs // SUB
    dk = dv = LANES

    @pl.when(c == 0)
    def _():
        for hh in range(hps):
            st_ref[hh] = s0_ref[0, hh].T

    gc_ref[...] = _tri_cumsum(f_ref[0], cs)
    rsub = lax.broadcasted_iota(jnp.int32, (SUB, 1), 0)

    def chunk(ci, carry):
        r0 = pl.multiple_of(ci * cs, cs)
        for hh in range(hps):
            q = q_ref[0, pl.ds(r0, cs), hh * dk:(hh + 1) * dk]
            k = k_ref[0, pl.ds(r0, cs), hh * dk:(hh + 1) * dk]
            iv = i_ref[0, pl.ds(r0, cs), hh * dv:(hh + 1) * dv]
            gcum = gc_ref[pl.ds(r0, cs), hh * dk:(hh + 1) * dk]
            st = st_ref[hh]
            st16 = st.astype(BF16)
            iv16 = iv.astype(BF16)
            o_inter = _dot_nt((q * jnp.exp(gcum)).astype(BF16), st16)
            for si in range(nsub):
                lo, hi = si * SUB, (si + 1) * SUB
                g_i = gcum[lo:hi]
                q_i = q[lo:hi]
                k_i = k[lo:hi]
                i_i = iv[lo:hi]
                acc = o_inter[lo:hi]
                if si > 0:
                    gref = gcum[lo - 1:lo]
                    qt = (q_i * jnp.exp(g_i - gref)).astype(BF16)
                    kt = (k[:lo] * jnp.exp(gref - gcum[:lo])).astype(BF16)
                    att = _dot_nt(qt, kt)
                    acc = acc + jnp.dot(att.astype(BF16), iv16[:lo], preferred_element_type=F32)
                for s in range(SUB):
                    e = jnp.exp(jnp.where(rsub >= s, g_i - g_i[s:s + 1], NEG_INF))
                    a = jnp.sum(q_i * k_i[s:s + 1] * e, axis=-1, keepdims=True)
                    acc = acc + a * i_i[s:s + 1]
                o_ref[pl.ds(r0 + lo, SUB), hh * dv:(hh + 1) * dv] = acc
            glast = gcum[cs - 1:cs]
            kh = (k * jnp.exp(glast - gcum)).astype(BF16)
            st_ref[hh] = st * jnp.exp(glast) + _dot_tn(iv16, kh)
        return carry

    lax.fori_loop(0, t_step // cs, chunk, 0)
    for hh in range(hps):
        o = o_ref[:, hh * dv:(hh + 1) * dv]
        y = o * lax.rsqrt(jnp.mean(o * o, axis=-1, keepdims=True) + EPS) * ng_ref[...]
        y_ref[0, :, hh * dv:(hh + 1) * dv] = (y * g_ref[0, :, hh * dv:(hh + 1) * dv]).astype(y_ref.dtype)

    @pl.when(c == pl.num_programs(2) - 1)
    def _():
        for hh in range(hps):
            s_ref[0, hh] = st_ref[hh].T


HGRN_HEADS_PER_STEP = 16


def _hgrn_recurrence(q, k, iv, logf, gate, norm_g, s0, t_step, cs):
    b, seq, hdm = q.shape
    dk = C_DK
    dv = hdm // C_HEADS
    hps = HGRN_HEADS_PER_STEP
    assert dk == LANES and dv == LANES and seq % t_step == 0 and t_step % cs == 0 and cs % SUB == 0
    assert C_HEADS % hps == 0
    kern = functools.partial(_hgrn_kernel, t_step=t_step, cs=cs, hps=hps)
    tok = pl.BlockSpec((1, t_step, hps * dv), lambda i, h, c: (i, c, h))
    st = pl.BlockSpec((1, hps, dk, dv), lambda i, h, c: (i, h, 0, 0))
    return pl.pallas_call(
        kern,
        grid=(b, C_HEADS // hps, seq // t_step),
        in_specs=[tok, tok, tok, tok, tok, pl.BlockSpec((1, dv), lambda i, h, c: (0, 0)), st],
        out_specs=[tok, st],
        out_shape=[jax.ShapeDtypeStruct((b, seq, hdm), BF16),
                   jax.ShapeDtypeStruct((b, C_HEADS, dk, dv), F32)],
        scratch_shapes=[pltpu.VMEM((hps, dv, dk), F32), pltpu.VMEM((t_step, hps * dk), F32),
                        pltpu.VMEM((t_step, hps * dv), F32)],
        compiler_params=_cparams("parallel", "parallel", "arbitrary"),
        name="hgrn_recurrence",
    )(q, k, iv, logf, gate, norm_g.reshape(1, dv), s0)


def _mlstm_kernel(q_ref, k_ref, v_ref, og_ref, gt_ref, ng_ref, c0_ref, n0_ref, m0_ref,
                  y_ref, c_ref, n_ref, m_ref, *, c, dk, dv, valid_len):
    @pl.when(pl.program_id(1) == 0)
    def _():
        c_ref[...] = c0_ref[...]
        n_ref[...] = n0_ref[...]
        m_ref[...] = m0_ref[...]

    gt = gt_ref[0]
    rowc = lax.broadcasted_iota(jnp.int32, (c, LANES), 0)
    lanec = lax.broadcasted_iota(jnp.int32, (c, LANES), 1)
    gt = jnp.where(rowc < valid_len, gt, jnp.where(lanec < D_HEADS, -1e30, 0.0))
    cum = _tri_cumsum(gt, c)
    gt_t = gt.T
    cum_t = cum.T
    r_i = lax.broadcasted_iota(jnp.int32, (c, c), 0)
    c_i = lax.broadcasted_iota(jnp.int32, (c, c), 1)
    tril = c_i <= r_i
    for h in range(D_HEADS):
        b_col = cum[:, D_HEADS + h:D_HEADS + h + 1]
        b_row = cum_t[D_HEADS + h:D_HEADS + h + 1, :]
        li_col = gt[:, h:h + 1]
        li_row = gt_t[h:h + 1, :]
        m_prev = m_ref[0, h][:, 0:1]
        n_row = n_ref[0, h]
        c_st = c_ref[0, h]
        qh = q_ref[0, :, h * dk:(h + 1) * dk]
        kh = k_ref[0, :, h * dk:(h + 1) * dk]
        vh16 = v_ref[0, :, h * dv:(h + 1) * dv].astype(BF16)
        qh16 = qh.astype(BF16)

        dmat = jnp.where(tril, b_col - b_row + li_row, NEG_INF)
        inter = b_col + m_prev
        m_t = jnp.maximum(inter, jnp.max(dmat, axis=-1, keepdims=True))
        w_inter = jnp.exp(inter - m_t)
        wqk = jnp.exp(dmat - m_t) * _dot_nt(qh16, kh.astype(BF16))
        num = w_inter * jnp.dot(qh16, c_st.astype(BF16), preferred_element_type=F32) \
            + jnp.dot(wqk.astype(BF16), vh16, preferred_element_type=F32)
        den = w_inter * jnp.sum(qh * n_row, axis=-1, keepdims=True) + jnp.sum(wqk, axis=-1, keepdims=True)
        hh = num / jnp.maximum(jnp.abs(den), jnp.exp(-m_t))

        b_last = b_col[c - 1:c, :]
        a_col = b_last - b_col + li_col
        m_new = jnp.maximum(b_last + m_prev, jnp.max(a_col, axis=0, keepdims=True))
        w_c = jnp.exp(b_last + m_prev - m_new)
        w_s = jnp.exp(a_col - m_new)
        ks = w_s * kh
        c_ref[0, h] = w_c * c_st + _dot_tn(ks.astype(BF16), vh16)
        n_ref[0, h] = w_c * n_row + jnp.sum(ks, axis=0, keepdims=True)
        m_ref[0, h] = jnp.broadcast_to(m_new, (1, LANES))

        y = hh * lax.rsqrt(jnp.mean(hh * hh, axis=-1, keepdims=True) + EPS) * ng_ref[:, h * dv:(h + 1) * dv]
        y_ref[0, :, h * dv:(h + 1) * dv] = (y * og_ref[0, :, h * dv:(h + 1) * dv]).astype(y_ref.dtype)


def _mlstm_recurrence(q, k, v, og, gates, norm_g, c0, n0, m0, c, valid_len):
    b, seq, _ = q.shape
    dk = q.shape[2] // D_HEADS
    dv = v.shape[2] // D_HEADS
    assert seq % c == 0 and (seq == c or valid_len == c)
    n0 = n0.reshape(b, D_HEADS, 1, dk)
    m0 = jnp.broadcast_to(m0.reshape(b, D_HEADS, 1, 1), (b, D_HEADS, 1, LANES))
    kern = functools.partial(_mlstm_kernel, c=c, dk=dk, dv=dv, valid_len=valid_len)

    def tok(w):
        return pl.BlockSpec((1, c, w), lambda i, j: (i, j, 0))

    c_spec = pl.BlockSpec((1, D_HEADS, dk, dv), lambda i, j: (i, 0, 0, 0))
    n_spec = pl.BlockSpec((1, D_HEADS, 1, dk), lambda i, j: (i, 0, 0, 0))
    m_spec = pl.BlockSpec((1, D_HEADS, 1, LANES), lambda i, j: (i, 0, 0, 0))
    y, c_out, n_out, m_out = pl.pallas_call(
        kern,
        grid=(b, seq // c),
        in_specs=[tok(D_HEADS * dk), tok(D_HEADS * dk), tok(D_HEADS * dv), tok(D_HEADS * dv), tok(LANES),
                  pl.BlockSpec((1, D_HEADS * dv), lambda i, j: (0, 0)), c_spec, n_spec, m_spec],
        out_specs=[tok(D_HEADS * dv), c_spec, n_spec, m_spec],
        out_shape=[jax.ShapeDtypeStruct((b, seq, D_HEADS * dv), BF16),
                   jax.ShapeDtypeStruct((b, D_HEADS, dk, dv), F32),
                   jax.ShapeDtypeStruct((b, D_HEADS, 1, dk), F32),
                   jax.ShapeDtypeStruct((b, D_HEADS, 1, LANES), F32)],
        compiler_params=_cparams("parallel", "arbitrary"),
        name="mlstm_recurrence",
    )(q, k, v, og, gates, norm_g.reshape(1, D_HEADS * dv), c0, n0, m0)
    return y, c_out, n_out.reshape(b, D_HEADS, dk), m_out[:, :, 0, 0]


def _ident(z):
    return (z,)


def _gelu(z):
    return (jax.nn.gelu(z, approximate=True),)


def _mix_gmlp(x, g, j, w_in, ln_g, ln_b, w_s, b_s, w_out):
    b, seq, d = x.shape
    width = w_in.shape[2] // 2
    chunk = w_s.shape[1]
    c = chunk if seq % chunk == 0 else seq
    x2 = x.reshape(b * seq, d)
    (z,) = _norm_matmul(x2, g, w_in, j, 0, 2 * width, _gelu, (F32,))
    a, v = _gmlp_gate(z.reshape(b * seq // c, c, 2 * width), ln_g, ln_b, w_s, b_s, c)
    y = _matmul_residual(a.reshape(b * seq, width), w_out, j, x2)
    return y.reshape(b, seq, d), v.reshape(b, seq, width)


def _moba_qkv(x, g, j, w_qkv):
    b, seq, d = x.shape
    hdm = w_qkv.shape[2] // 3
    x2 = x.reshape(b * seq, d)
    qkv = _norm_matmul_parts(x2, g, w_qkv, j, 0, [(hdm, _ident, (F32,), ())] * 3)
    return [p[0].reshape(b, seq, hdm) for p in qkv]


def _hgrn_project(x2, g, j, w_in, lb):
    hdm = w_in.shape[2] // 4
    lb = lb.reshape(1, hdm)

    def silu(z):
        return (z * _sigmoid(z),)

    def forget(z, lbv):
        logf = jnp.log(lbv + (1.0 - lbv) * _sigmoid(z))
        return logf, (1.0 - lbv) * _sigmoid(-z)

    (q,), (logf, k), (iv,), (gate,) = _norm_matmul_parts(
        x2, g, w_in, j, 0, [(hdm, silu, (F32,), ()), (hdm, forget, (F32, F32), (lb,)),
                            (hdm, _ident, (F32,), ()), (hdm, silu, (F32,), ())])
    return q, k, iv, logf, gate


def _mix_hgrn(x, g, j, w_in, lb, norm_g, w_out, s0):
    b, seq, d = x.shape
    x2 = x.reshape(b * seq, d)
    parts = [p.reshape(b, seq, -1) for p in _hgrn_project(x2, g, j, w_in, lb)]
    if seq % 256 == 0:
        t_step, cs, pad = 256, 64, 0
    else:
        t_step = cs = -(-seq // SEQ_PAD) * SEQ_PAD
        pad = t_step - seq
        parts = [jnp.pad(p, ((0, 0), (0, pad), (0, 0))) for p in parts]
    y, s = _hgrn_recurrence(*parts, norm_g, s0, t_step, cs)
    y = y[:, :seq].reshape(b * seq, -1)
    return _matmul_residual(y, w_out, j, x2).reshape(b, seq, d), s


def _mix_mlstm(x, g, j, w_in, w_gates, b_gates, norm_g, w_out, c0, n0, m0):
    b, seq, d = x.shape
    dk = c0.shape[2]
    dv = c0.shape[3]
    nq, nv = D_HEADS * dk, D_HEADS * dv
    x2 = x.reshape(b * seq, d)

    def kscale(z):
        return (z * (dk ** -0.5),)

    def ogate(z):
        return (_sigmoid(z),)

    def gates_fn(z, bias):
        zz = z + bias
        lane = lax.broadcasted_iota(jnp.int32, zz.shape, 1)
        return (jnp.where(lane < D_HEADS, zz, _log_sigmoid(zz)),)

    (q,), (k,), (v,), (og,) = _norm_matmul_parts(
        x2, g, w_in, j, 0, [(nq, _ident, (BF16,), ()), (nq, kscale, (BF16,), ()),
                            (nv, _ident, (BF16,), ()), (nv, ogate, (BF16,), ())])
    (gt,) = _norm_matmul(x2, g, w_gates, 0, 0, LANES, gates_fn, (F32,), extras=(b_gates,))
    parts = [p.reshape(b, seq, -1) for p in (q, k, v, og, gt)]
    if seq % 256 == 0:
        c, valid = 256, 256
    else:
        c = -(-seq // SEQ_PAD) * SEQ_PAD
        valid = seq
        parts = [jnp.pad(p, ((0, 0), (0, c - seq), (0, 0))) for p in parts]
    y, c_out, n_out, m_out = _mlstm_recurrence(*parts, norm_g, c0, n0, m0, c, valid)
    y = y[:, :seq].reshape(b * seq, nv)
    return _matmul_residual(y, w_out, j, x2).reshape(b, seq, d), c_out, n_out, m_out


def kernel(x_prompt, x_sample, cache_k, cache_v, page_table, state_hgrn, state_mlstm_c, state_mlstm_n, state_mlstm_m, norm_mix, norm_ffn, norm_final, w_ffn_up, w_ffn_down, a_w_in, a_ln_g, a_ln_b, a_w_s, a_b_s, a_w_out, b_w_qkv, b_w_out, c_w_in, c_lower_bound, c_norm_g, c_w_out, d_w_in, d_b_gates, d_norm_g, d_w_out):
    depth = norm_mix.shape[0]
    bsz, seq, d = x_prompt.shape
    dbsz, dseq, _ = x_sample.shape
    slopes = jnp.asarray(2.0 ** (-8.0 * np.arange(1, B_HEADS + 1) / B_HEADS), F32)
    lbs = jax.nn.softmax(c_lower_bound.astype(F32), axis=0)
    lbs = jnp.cumsum(lbs, axis=0) - lbs[0]
    hdm = b_w_qkv.shape[2] // 3

    xp, xs = x_prompt, x_sample
    outs = {k: [] for k in ("av", "kp", "vp", "ks", "vs", "hp", "hs", "cp", "np", "mp", "cs", "ns", "ms")}
    for layer in range(depth):
        kind = layer % N_MIXERS
        j = layer // N_MIXERS
        g = norm_mix[layer]
        if kind == 0:
            args = (j, a_w_in, a_ln_g[j], a_ln_b[j], a_w_s[j], a_b_s[j], a_w_out)
            xp, _ = _mix_gmlp(xp, g, *args)
            xs, vrow = _mix_gmlp(xs, g, *args)
            outs["av"].append(vrow)
        elif kind == 1:
            qp, kp, vp = _moba_qkv(xp, g, j, b_w_qkv)
            op = _moba_prompt(qp, kp, vp, slopes)
            xp = _matmul_residual(op.reshape(bsz * seq, hdm), b_w_out, j,
                                  xp.reshape(bsz * seq, d)).reshape(bsz, seq, d)
            qs, ks, vs = _moba_qkv(xs, g, j, b_w_qkv)
            os_ = _moba_sample(qs, ks, vs, cache_k, cache_v, j, page_table, slopes)
            xs = _matmul_residual(os_.reshape(dbsz * dseq, hdm), b_w_out, j,
                                  xs.reshape(dbsz * dseq, d)).reshape(dbsz, dseq, d)
            hd = hdm // B_HEADS
            outs["kp"].append(kp.reshape(bsz, seq, B_HEADS, hd))
            outs["vp"].append(vp.reshape(bsz, seq, B_HEADS, hd))
            outs["ks"].append(ks.reshape(dbsz, dseq, B_HEADS, hd))
            outs["vs"].append(vs.reshape(dbsz, dseq, B_HEADS, hd))
        elif kind == 2:
            args = (j, c_w_in, lbs[layer], c_norm_g[j], c_w_out)
            s0 = jnp.zeros((bsz,) + state_hgrn.shape[2:], F32)
            xp, sp = _mix_hgrn(xp, g, *args, s0)
            xs, ss = _mix_hgrn(xs, g, *args, state_hgrn[j])
            outs["hp"].append(sp)
            outs["hs"].append(ss)
        else:
            ng = 2 * D_HEADS * (state_mlstm_c.shape[3] + state_mlstm_c.shape[4])
            w_gates = jnp.pad(d_w_in[j][:, ng:], ((0, 0), (0, LANES - 2 * D_HEADS)))[None]
            b_gates = jnp.pad(d_b_gates[j], (0, LANES - 2 * D_HEADS)).reshape(1, LANES)
            args = (j, d_w_in, w_gates, b_gates, d_norm_g[j], d_w_out)
            zc = jnp.zeros((bsz,) + state_mlstm_c.shape[2:], F32)
            zn = jnp.zeros((bsz,) + state_mlstm_n.shape[2:], F32)
            zm = jnp.zeros((bsz,) + state_mlstm_m.shape[2:], F32)
            xp, cp, np_, mp = _mix_mlstm(xp, g, *args, zc, zn, zm)
            xs, cs, ns, ms = _mix_mlstm(xs, g, *args, state_mlstm_c[j], state_mlstm_n[j], state_mlstm_m[j])
            for key, val in zip(("cp", "np", "mp", "cs", "ns", "ms"), (cp, np_, mp, cs, ns, ms)):
                outs[key].append(val)
        xp = _ffn(xp.reshape(bsz * seq, d), norm_ffn[layer], w_ffn_up, w_ffn_down, layer).reshape(bsz, seq, d)
        xs = _ffn(xs.reshape(dbsz * dseq, d), norm_ffn[layer], w_ffn_up, w_ffn_down, layer).reshape(dbsz, dseq, d)
    y_prompt = _rmsnorm(xp.reshape(bsz * seq, d), norm_final).reshape(bsz, seq, d)
    y_sample = _rmsnorm(xs.reshape(dbsz * dseq, d), norm_final).reshape(dbsz, dseq, d)
    st = {k: jnp.stack(v) for k, v in outs.items()}
    return (y_prompt, y_sample, st["av"], st["kp"], st["vp"], st["ks"], st["vs"], st["hp"], st["hs"],
            st["cp"], st["np"], st["mp"], st["cs"], st["ns"], st["ms"])
```

```python
import functools

import jax
import jax.numpy as jnp
import numpy as np
from jax import lax
from jax.experimental import pallas as pl
from jax.experimental.pallas import tpu as pltpu

F32 = jnp.float32
BF16 = jnp.bfloat16
EPS = 1e-6
NEG_INF = float("-inf")

A_GROUPS = 8
B_HEADS = 16
B_BLOCK = 256
B_TOPK = 3
C_HEADS = 16
C_DK = 128
D_HEADS = 4
N_MIXERS = 4

LANES = 128
SUBLANES = 8
VMEM_LIMIT_BYTES = 56 * 1024 * 1024
SUB = 16
SEQ_PAD = 16
MM_TILE_M = 1024
MM_TILE_N = 512
FFN_TILE_F = 512
RES_TILE_M = 2048
RES_TILE_N = 512


def _cparams(*sem):
    return pltpu.CompilerParams(dimension_semantics=sem, vmem_limit_bytes=VMEM_LIMIT_BYTES)


def _sigmoid(x):
    return 1.0 / (1.0 + jnp.exp(-x))


def _log_sigmoid(x):
    return jnp.minimum(x, 0.0) - jnp.log(1.0 + jnp.exp(-jnp.abs(x)))


def _split3(x):
    hi = x.astype(BF16)
    r1 = x - hi.astype(F32)
    mid = r1.astype(BF16)
    lo = (r1 - mid.astype(F32)).astype(BF16)
    return hi, mid, lo


def _tri_cumsum(x, block):
    n = x.shape[0]
    r = lax.broadcasted_iota(jnp.int32, (n, n), 0)
    c = lax.broadcasted_iota(jnp.int32, (n, n), 1)
    tri = jnp.where((c <= r) & ((r // block) == (c // block)), 1.0, 0.0).astype(BF16)
    hi, mid, lo = _split3(x)
    out = jnp.dot(tri, lo, preferred_element_type=F32)
    out = out + jnp.dot(tri, mid, preferred_element_type=F32)
    return out + jnp.dot(tri, hi, preferred_element_type=F32)


def _dot_nt(a, b):
    return lax.dot_general(a, b, (((1,), (1,)), ((), ())), preferred_element_type=F32)


def _dot_tn(a, b):
    return lax.dot_general(a, b, (((0,), (0,)), ((), ())), preferred_element_type=F32)


def _dot_nt_f32(a, b):
    a0, a1, a2 = _split3(a)
    b0, b1, b2 = _split3(b)
    out = _dot_nt(a1, b1) + _dot_nt(a0, b2) + _dot_nt(a2, b0)
    out = out + _dot_nt(a0, b1) + _dot_nt(a1, b0)
    return out + _dot_nt(a0, b0)


def _norm_mm_kernel(x_ref, g_ref, w_ref, *rest, parts):
    n_extra = sum(p[3] for p in parts)
    n_out = sum(p[4] for p in parts)
    extras, outs, xn_ref = rest[:n_extra], rest[n_extra:n_extra + n_out], rest[n_extra + n_out]
    j = pl.program_id(1)

    @pl.when(j == 0)
    def _():
        x = x_ref[...]
        y = x * lax.rsqrt(jnp.mean(x * x, axis=-1, keepdims=True) + EPS)
        xn_ref[...] = (y * g_ref[...]).astype(BF16)

    z = jnp.dot(xn_ref[...], w_ref[...].astype(BF16), preferred_element_type=F32)
    e0 = o0 = 0
    for jb0, nb, epilogue, ne, no in parts:
        def finish(epilogue=epilogue, ex=extras[e0:e0 + ne], os=outs[o0:o0 + no]):
            for o_ref, r in zip(os, epilogue(z, *[e[...] for e in ex])):
                o_ref[...] = r.astype(o_ref.dtype)

        if len(parts) == 1:
            finish()
        else:
            pl.when((j >= jb0) & (j < jb0 + nb))(finish)
        e0, o0 = e0 + ne, o0 + no


def _norm_matmul_parts(x, g, w, layer, col0, parts):
    m, d = x.shape
    tm = min(m, MM_TILE_M)
    tn = min(min(p[0] for p in parts), MM_TILE_N)
    assert m % tm == 0 and col0 % tn == 0 and all(p[0] % tn == 0 for p in parts)
    jw = col0 // tn
    kparts, extra_specs, out_specs, out_shapes, extra_args = [], [], [], [], []
    jb0 = 0
    for ncols, epilogue, out_dtypes, extras in parts:
        nb = ncols // tn

        def col_block(i, j, jb0=jb0, nb=nb):
            return jnp.clip(j - jb0, 0, nb - 1)

        kparts.append((jb0, nb, epilogue, len(extras), len(out_dtypes)))
        extra_specs += [pl.BlockSpec((1, tn), lambda i, j, cb=col_block: (0, cb(i, j))) for _ in extras]
        out_specs += [pl.BlockSpec((tm, tn), lambda i, j, cb=col_block: (i, cb(i, j))) for _ in out_dtypes]
        out_shapes += [jax.ShapeDtypeStruct((m, ncols), dt) for dt in out_dtypes]
        extra_args += list(extras)
        jb0 += nb
    res = pl.pallas_call(
        functools.partial(_norm_mm_kernel, parts=tuple(kparts)),
        grid=(m // tm, jb0),
        in_specs=[pl.BlockSpec((tm, d), lambda i, j: (i, 0)),
                  pl.BlockSpec((1, d), lambda i, j: (0, 0)),
                  pl.BlockSpec((None, d, tn), lambda i, j: (layer, 0, j + jw))] + extra_specs,
        out_specs=out_specs,
        out_shape=out_shapes,
        scratch_shapes=[pltpu.VMEM((tm, d), BF16)],
        compiler_params=_cparams("parallel", "arbitrary"),
        name="norm_matmul",
    )(x, g.reshape(1, d), w, *extra_args)
    out, o0 = [], 0
    for p in parts:
        out.append(tuple(res[o0:o0 + len(p[2])]))
        o0 += len(p[2])
    return out


def _norm_matmul(x, g, w, layer, col0, ncols, epilogue, out_dtypes, extras=()):
    return _norm_matmul_parts(x, g, w, layer, col0, [(ncols, epilogue, out_dtypes, extras)])[0]


def _mm_res_kernel(a_ref, w_ref, r_ref, o_ref):
    o_ref[...] = r_ref[...] + jnp.dot(a_ref[...].astype(BF16), w_ref[...].astype(BF16),
                                      preferred_element_type=F32)


def _matmul_residual(a, w, layer, res):
    m, k = a.shape
    n = w.shape[2]
    tm = min(m, RES_TILE_M)
    tn = min(n, RES_TILE_N)
    assert m % tm == 0 and n % tn == 0
    return pl.pallas_call(
        _mm_res_kernel,
        grid=(m // tm, n // tn),
        in_specs=[pl.BlockSpec((tm, k), lambda i, j: (i, 0)),
                  pl.BlockSpec((None, k, tn), lambda i, j: (layer, 0, j)),
                  pl.BlockSpec((tm, tn), lambda i, j: (i, j))],
        out_specs=pl.BlockSpec((tm, tn), lambda i, j: (i, j)),
        out_shape=jax.ShapeDtypeStruct((m, n), F32),
        compiler_params=_cparams("parallel", "parallel"),
        name="matmul_residual",
    )(a, w, res)


def _ffn_kernel(x_ref, g_ref, wu_ref, wd_ref, o_ref, xn_ref):
    f = pl.program_id(1)

    @pl.when(f == 0)
    def _():
        x = x_ref[...]
        y = x * lax.rsqrt(jnp.mean(x * x, axis=-1, keepdims=True) + EPS)
        xn_ref[...] = (y * g_ref[...]).astype(BF16)
        o_ref[...] = x

    h = jnp.maximum(jnp.dot(xn_ref[...], wu_ref[...].astype(BF16), preferred_element_type=F32), 0.0)
    o_ref[...] += jnp.dot((h * h).astype(BF16), wd_ref[...].astype(BF16), preferred_element_type=F32)


def _ffn(x, g, w_up, w_down, layer):
    m, d = x.shape
    dff = w_up.shape[2]
    tm = min(m, MM_TILE_M)
    tf = min(dff, FFN_TILE_F)
    assert m % tm == 0 and dff % tf == 0
    return pl.pallas_call(
        _ffn_kernel,
        grid=(m // tm, dff // tf),
        in_specs=[pl.BlockSpec((tm, d), lambda i, f: (i, 0), pipeline_mode=pl.Buffered(1)),
                  pl.BlockSpec((1, d), lambda i, f: (0, 0)),
                  pl.BlockSpec((None, d, tf), lambda i, f: (layer, 0, f)),
                  pl.BlockSpec((None, tf, d), lambda i, f: (layer, f, 0))],
        out_specs=pl.BlockSpec((tm, d), lambda i, f: (i, 0)),
        out_shape=jax.ShapeDtypeStruct((m, d), F32),
        scratch_shapes=[pltpu.VMEM((tm, d), BF16)],
        compiler_params=_cparams("parallel", "arbitrary"),
        name="ffn",
    )(x, g.reshape(1, d), w_up, w_down)


def _rmsnorm_kernel(x_ref, g_ref, o_ref):
    x = x_ref[...]
    o_ref[...] = x * lax.rsqrt(jnp.mean(x * x, axis=-1, keepdims=True) + EPS) * g_ref[...]


def _rmsnorm(x, g):
    m, d = x.shape
    tm = min(m, 512)
    return pl.pallas_call(
        _rmsnorm_kernel,
        grid=(m // tm,),
        in_specs=[pl.BlockSpec((tm, d), lambda i: (i, 0)), pl.BlockSpec((1, d), lambda i: (0, 0))],
        out_specs=pl.BlockSpec((tm, d), lambda i: (i, 0)),
        out_shape=jax.ShapeDtypeStruct((m, d), F32),
        compiler_params=_cparams("parallel"),
        name="final_rmsnorm",
    )(x, g.reshape(1, d))


def _gmlp_gate_kernel(z_ref, lg_ref, lb_ref, ws_ref, bs_ref, a_ref, v_ref, *, c, width):
    gw = width // A_GROUPS
    z = z_ref[0]
    u = z[:, :width]
    vr = z[:, width:]
    mu = jnp.mean(vr, axis=-1, keepdims=True)
    vc = vr - mu
    v = vc * lax.rsqrt(jnp.mean(vc * vc, axis=-1, keepdims=True) + EPS) * lg_ref[...] + lb_ref[...]
    v_ref[0] = v
    row = lax.broadcasted_iota(jnp.int32, (c, c), 0)
    col = lax.broadcasted_iota(jnp.int32, (c, c), 1)
    bs = bs_ref[...]
    for g in range(A_GROUPS):
        ws = jnp.where(col <= row, ws_ref[g], 0.0)
        vg = v[:, g * gw:(g + 1) * gw]
        if c >= 2 * SUBLANES:
            s = jnp.dot(ws.astype(BF16), vg.astype(BF16), preferred_element_type=F32)
        else:
            s = jnp.zeros((c, gw), F32)
            for t in range(c):
                s = s + ws[:, t:t + 1] * vg[t:t + 1, :]
        s = s + bs[:, g:g + 1]
        a_ref[0, :, g * gw:(g + 1) * gw] = (u[:, g * gw:(g + 1) * gw] * s).astype(a_ref.dtype)


def _gmlp_gate(z, ln_g, ln_b, w_s, b_s, c):
    nb, _, w2 = z.shape
    width = w2 // 2
    ws = w_s[:, :c, :c]
    bs_t = b_s[:, :c].T
    kern = functools.partial(_gmlp_gate_kernel, c=c, width=width)
    return pl.pallas_call(
        kern,
        grid=(nb,),
        in_specs=[pl.BlockSpec((1, c, w2), lambda i: (i, 0, 0)),
                  pl.BlockSpec((1, width), lambda i: (0, 0)),
                  pl.BlockSpec((1, width), lambda i: (0, 0)),
                  pl.BlockSpec((A_GROUPS, c, c), lambda i: (0, 0, 0)),
                  pl.BlockSpec((c, A_GROUPS), lambda i: (0, 0))],
        out_specs=[pl.BlockSpec((1, c, width), lambda i: (i, 0, 0)),
                   pl.BlockSpec((1, c, width), lambda i: (i, 0, 0))],
        out_shape=[jax.ShapeDtypeStruct((nb, c, width), BF16 if c % (2 * SUBLANES) == 0 else F32),
                   jax.ShapeDtypeStruct((nb, c, width), F32)],
        compiler_params=_cparams("parallel"),
        name="gmlp_gate",
    )(z, ln_g.reshape(1, width), ln_b.reshape(1, width), ws, bs_t)


MASKED = -1e30
V_PAD_ROWS = 16


MOBA_HEADS_PER_STEP = 2


def _moba_prompt_kernel(slope_ref, q_ref, k_ref, v_ref, o_ref, kaug_ref, vt_ref, means_ref, mask_ref,
                        *, nblk, hd, hps):
    blk = B_BLOCK
    scale = hd ** -0.5
    nbp = means_ref.shape[1]
    heads = range(hps)
    slopes = [slope_ref[pl.program_id(1) * hps + hh] for hh in heads]

    ali = nbp + SUBLANES
    lane_blk = lax.broadcasted_iota(jnp.int32, (blk, hd), 1)
    key_off = lax.broadcasted_iota(jnp.int32, (blk, hd), 0).astype(F32)
    ones_rows = jnp.where(lax.broadcasted_iota(jnp.int32, (V_PAD_ROWS, blk), 0) == 0, 1.0, 0.0).astype(BF16)
    key_i = lax.broadcasted_iota(jnp.int32, (blk, blk), 0)
    qry_i = lax.broadcasted_iota(jnp.int32, (blk, blk), 1)
    mask_ref[0:blk, :] = jnp.zeros((blk, blk), F32)
    mask_ref[blk:2 * blk, :] = jnp.where(qry_i >= key_i, 0.0, NEG_INF)
    row8 = lax.broadcasted_iota(jnp.int32, (SUBLANES, blk), 0)
    means_ref[...] = jnp.zeros_like(means_ref)
    slope_rows = []
    for hh in heads:
        cols = slice(hh * hd, (hh + 1) * hd)
        kaug_ref[hh, 0:blk, 0:hd] = jnp.zeros((blk, hd), BF16)
        kaug_ref[hh, 0:blk, hd:2 * hd] = jnp.where(lane_blk == nbp, 1.0, 0.0).astype(BF16)
        vt_ref[hh, 0] = jnp.zeros((hd + V_PAD_ROWS, blk), BF16)
        for n in range(nblk):
            kb = k_ref[0, n * blk:(n + 1) * blk, cols]
            means_ref[hh, n:n + 1, :] = jnp.sum(kb, axis=0, keepdims=True) * (1.0 / blk)
            kaug_ref[hh, (n + 1) * blk:(n + 2) * blk, 0:hd] = kb.astype(BF16)
            extra = jnp.where(lane_blk == n, 1.0, 0.0)
            extra = jnp.where((lane_blk == ali) | (lane_blk == ali + 1), key_off, extra)
            extra = jnp.where((lane_blk == ali + 2) | (lane_blk == ali + 3), float(n), extra)
            kaug_ref[hh, (n + 1) * blk:(n + 2) * blk, hd:2 * hd] = extra.astype(BF16)
            vt_ref[hh, n + 1, 0:hd, :] = v_ref[0, n * blk:(n + 1) * blk, cols].T.astype(BF16)
            vt_ref[hh, n + 1, hd:hd + V_PAD_ROWS, :] = ones_rows
        s_full = jnp.full((SUBLANES, blk), slopes[hh], F32)
        s_hi = s_full.astype(BF16).astype(F32)
        s_lo = s_full - s_hi
        rows = jnp.where(row8 == 0, s_hi, jnp.where(row8 == 1, s_lo, 0.0))
        rows = jnp.where(row8 == 2, blk * s_hi, jnp.where(row8 == 3, blk * s_lo, rows))
        slope_rows.append(rows)
    mean_parts = [_split3(means_ref[hh])[:2] for hh in heads]
    rown = lax.broadcasted_iota(jnp.int32, (nbp, blk), 0)

    def update(st, s, vt):
        m, acc = st
        m_new = jnp.maximum(m, jnp.max(s, axis=0, keepdims=True))
        p = jnp.exp(s - m_new)
        return m_new, jnp.exp(m - m_new) * acc + jnp.dot(vt, p.astype(BF16), preferred_element_type=F32)

    def query_side(qb):
        q0 = pl.multiple_of(qb * blk, blk)
        qaugs = []
        for hh in heads:
            q = q_ref[0, pl.ds(q0, blk), hh * hd:(hh + 1) * hd]
            q_hi, q_mid, _ = _split3(q)
            mean_hi, mean_mid = mean_parts[hh]
            sc = _dot_nt(mean_hi, q_mid) + _dot_nt(mean_mid, q_hi) + _dot_nt(mean_hi, q_hi)
            sc = jnp.where(rown < qb, sc, NEG_INF)
            cnt = jnp.zeros((nbp, blk), F32)
            for n2 in range(nblk):
                rn = sc[n2:n2 + 1, :]
                ahead = (rn > sc) | ((rn == sc) & (rown > n2))
                cnt = cnt + jnp.where(ahead, 1.0, 0.0)
            keep = ((cnt < B_TOPK) & (rown < qb)) | (rown == qb)
            selb = jnp.where(keep, 0.0, MASKED)
            rest = jnp.zeros((hd - nbp - 2 * SUBLANES, blk), F32)
            qaug = jnp.concatenate([(q * scale).T, selb, jnp.full((SUBLANES, blk), MASKED, F32),
                                    slope_rows[hh], rest], axis=0)
            qaugs.append(qaug.astype(BF16))
        return tuple(qaugs)

    def q_block(qb, qaugs):
        q0 = pl.multiple_of(qb * blk, blk)
        qaugs_next = query_side(jnp.minimum(qb + 1, nblk - 1))

        def pair_scores(hh, slot):
            k0 = pl.multiple_of(jnp.maximum(slot, 0) * blk, blk)
            return jnp.dot(kaug_ref[hh, pl.ds(k0, 2 * blk), :], qaugs[hh], preferred_element_type=F32)

        def pair_values(hh, slot):
            return jnp.concatenate([vt_ref[hh, slot], vt_ref[hh, slot + 1]], axis=1)

        first = tuple(pair_scores(hh, qb - 2) for hh in heads)
        state = []
        for hh in heads:
            s = pair_scores(hh, qb) + mask_ref[...]
            m = jnp.max(s, axis=0, keepdims=True)
            acc = jnp.dot(pair_values(hh, qb), jnp.exp(s - m).astype(BF16), preferred_element_type=F32)
            state.append((m, acc))

        def two_blocks(i, carry):
            st, scores = carry
            slot = qb - 2 - 2 * i
            nxt = tuple(pair_scores(hh, slot - 2) for hh in heads)
            return tuple(update(st[hh], scores[hh], pair_values(hh, slot)) for hh in heads), nxt

        state, _ = lax.fori_loop(0, qb // 2, two_blocks, (tuple(state), first))
        for hh in heads:
            acc = state[hh][1]
            o_t = acc[0:hd, :] / acc[hd:hd + 1, :]
            o_ref[0, pl.ds(q0, blk), hh * hd:(hh + 1) * hd] = o_t.T.astype(o_ref.dtype)
        return qaugs_next

    lax.fori_loop(0, nblk, q_block, query_side(jnp.int32(0)))


def _moba_prompt(q, k, v, slopes):
    b, seq, hdm = q.shape
    hd = hdm // B_HEADS
    hps = MOBA_HEADS_PER_STEP
    assert seq % B_BLOCK == 0 and hd == LANES and B_HEADS % hps == 0
    nblk = seq // B_BLOCK
    nbp = -(-nblk // SUBLANES) * SUBLANES
    assert nbp + 2 * SUBLANES <= hd
    kern = functools.partial(_moba_prompt_kernel, nblk=nblk, hd=hd, hps=hps)
    spec = pl.BlockSpec((1, seq, hps * hd), lambda i, h: (i, 0, h))
    return pl.pallas_call(
        kern,
        grid=(b, B_HEADS // hps),
        in_specs=[pl.BlockSpec(memory_space=pltpu.SMEM), spec, spec, spec],
        out_specs=spec,
        out_shape=jax.ShapeDtypeStruct((b, seq, hdm), BF16),
        scratch_shapes=[pltpu.VMEM((hps, seq + B_BLOCK, 2 * hd), BF16),
                        pltpu.VMEM((hps, nblk + 1, hd + V_PAD_ROWS, B_BLOCK), BF16),
                        pltpu.VMEM((hps, nbp, hd), F32),
                        pltpu.VMEM((2 * B_BLOCK, B_BLOCK), F32)],
        compiler_params=_cparams("parallel", "parallel"),
        name="moba_prompt",
    )(slopes, q, k, v)


MEAN_PAGES_PER_STEP = 8


def _page_mean_kernel(pt_ref, *refs, ppb, pps):
    o_ref = refs[pps]
    for blk in range(pps // ppb):
        s = jnp.sum(refs[blk * ppb][0, 0], axis=0)
        for pg in range(1, ppb):
            s = s + jnp.sum(refs[blk * ppb + pg][0, 0], axis=0)
        o_ref[0, blk] = s * (1.0 / B_BLOCK)


def _cache_block_means(cache, layer, page_table):
    _, npool, page, heads, hd = cache.shape
    db, npg = page_table.shape
    ppb = B_BLOCK // page
    nblk = npg // ppb
    pps = MEAN_PAGES_PER_STEP
    while npg % pps:
        pps //= 2
    assert pps % ppb == 0
    kern = functools.partial(_page_mean_kernel, ppb=ppb, pps=pps)

    def page_spec(r):
        return pl.BlockSpec((1, 1, page, heads, hd), lambda i, p, pt: (layer, pt[i, p * pps + r], 0, 0, 0))

    return pl.pallas_call(
        kern,
        grid_spec=pltpu.PrefetchScalarGridSpec(
            num_scalar_prefetch=1,
            grid=(db, npg // pps),
            in_specs=[page_spec(r) for r in range(pps)],
            out_specs=pl.BlockSpec((1, pps // ppb, heads, hd), lambda i, p, pt: (i, p, 0, 0)),
        ),
        out_shape=jax.ShapeDtypeStruct((db, nblk, heads, hd), F32),
        compiler_params=_cparams("parallel", "parallel"),
        name="moba_cache_means",
    )(page_table, *([cache] * pps))


def _moba_select_kernel(q_ref, mean_ref, o_ref, *, s_len, hd, n_valid):
    rows, hdm = q_ref.shape[1], q_ref.shape[2]
    nblk = mean_ref.shape[1]
    r_i = lax.broadcasted_iota(jnp.int32, (rows, hdm), 0)
    c_i = lax.broadcasted_iota(jnp.int32, (rows, hdm), 1)
    qx = jnp.where((c_i // hd) == (r_i // s_len), q_ref[0], 0.0)
    sc = _dot_nt_f32(qx, mean_ref[0])
    lane = lax.broadcasted_iota(jnp.int32, (rows, nblk), 1).astype(F32)
    out_lane = lax.broadcasted_iota(jnp.int32, (rows, LANES), 1)
    out = jnp.zeros((rows, LANES), F32)
    for t in range(n_valid):
        mx = jnp.max(sc, axis=-1, keepdims=True)
        idx = jnp.min(jnp.where(sc == mx, lane, float(nblk)), axis=-1, keepdims=True)
        out = jnp.where(out_lane == t, idx, out)
        sc = jnp.where(lane == idx, NEG_INF, sc)
    o_ref[0] = out.astype(jnp.int32)


def _moba_select(q_rep, means, s_len, n_valid):
    db, rows, hdm = q_rep.shape
    nblk = means.shape[1]
    kern = functools.partial(_moba_select_kernel, s_len=s_len, hd=hdm // B_HEADS, n_valid=n_valid)
    return pl.pallas_call(
        kern,
        grid=(db,),
        in_specs=[pl.BlockSpec((1, rows, hdm), lambda i: (i, 0, 0)),
                  pl.BlockSpec((1, nblk, hdm), lambda i: (i, 0, 0))],
        out_specs=pl.BlockSpec((1, rows, LANES), lambda i: (i, 0, 0)),
        out_shape=jax.ShapeDtypeStruct((db, rows, LANES), jnp.int32),
        compiler_params=_cparams("parallel"),
        name="moba_sample_select",
    )(q_rep, means)


def _moba_sample_kernel(phys_ref, idx_ref, slope_ref, q_ref, kn_ref, vn_ref, ck_hbm, cv_hbm, o_ref,
                        kbuf, vbuf, sems, q8_ref, *, layer, s_len, hd, past, n_valid, nsel, ppb):
    npages = nsel * ppb
    g = pl.program_id(0)
    h = g % B_HEADS
    slot = g % 2
    slope = slope_ref[h]
    scale = hd ** -0.5
    page = kbuf.shape[2]
    row8 = lax.broadcasted_iota(jnp.int32, (SUBLANES, 1), 0)

    def page_copies(step, to_slot):
        head = step % B_HEADS
        out = []
        for r in range(npages):
            pool_row = phys_ref[step * npages + r]
            out.append(pltpu.make_async_copy(ck_hbm.at[layer, pool_row, :, head, :], kbuf.at[to_slot, r],
                                             sems.at[0, to_slot]))
            out.append(pltpu.make_async_copy(cv_hbm.at[layer, pool_row, :, head, :], vbuf.at[to_slot, r],
                                             sems.at[1, to_slot]))
        return out

    @pl.when(g == 0)
    def _():
        for cp in page_copies(g, slot):
            cp.start()

    @pl.when(g + 1 < pl.num_programs(0))
    def _():
        for cp in page_copies(g + 1, 1 - slot):
            cp.start()

    for cp in page_copies(g, slot):
        cp.wait()
    k_refs = [kbuf.at[slot, r] for r in range(npages)]
    v_refs = [vbuf.at[slot, r] for r in range(npages)]

    q8_ref[...] = jnp.zeros_like(q8_ref)
    q8_ref[0:s_len, :] = q_ref[0] * scale
    q8 = q8_ref[...]
    kn = kn_ref[0]
    vn = vn_ref[0]
    s_cols = []
    for c in range(s_len):
        sc = jnp.sum(q8 * kn[c:c + 1, :], axis=-1, keepdims=True) - slope * (row8 - c).astype(F32)
        s_cols.append(jnp.where(row8 >= c, sc, NEG_INF))
    m = s_cols[0]
    for c in range(1, s_len):
        m = jnp.maximum(m, s_cols[c])

    qs = q8.astype(BF16)
    col = lax.broadcasted_iota(jnp.int32, (SUBLANES, ppb * page), 1)
    row = lax.broadcasted_iota(jnp.int32, (SUBLANES, ppb * page), 0)
    live = [j for j in range(nsel) if j % B_TOPK < n_valid]
    tiles = []
    for j in live:
        blk_id = idx_ref[g * nsel + j]
        kb = jnp.concatenate([k_refs[j * ppb + pg][...] for pg in range(ppb)], axis=0).astype(BF16)
        dist = (past + row - blk_id * B_BLOCK - col).astype(F32)
        s = jnp.where(row == j // B_TOPK, _dot_nt(qs, kb) - slope * dist, NEG_INF)
        tiles.append(s)
        m = jnp.maximum(m, jnp.max(s, axis=-1, keepdims=True))

    l = jnp.zeros((SUBLANES, 1), F32)
    acc = jnp.zeros((SUBLANES, hd), F32)
    for c in range(s_len):
        p = jnp.exp(s_cols[c] - m)
        l = l + p
        acc = acc + p * vn[c:c + 1, :]
    for j, s in zip(live, tiles):
        p = jnp.exp(s - m)
        vb = jnp.concatenate([v_refs[j * ppb + pg][...] for pg in range(ppb)], axis=0).astype(BF16)
        l = l + jnp.sum(p, axis=-1, keepdims=True)
        acc = acc + jnp.dot(p.astype(BF16), vb, preferred_element_type=F32)
    o_ref[0] = (acc / l)[0:s_len, :]


def _moba_sample(q, k_new, v_new, cache_k, cache_v, layer, page_table, slopes):
    db, s_len, hdm = q.shape
    hd = hdm // B_HEADS
    page = cache_k.shape[2]
    npg = page_table.shape[1]
    past = npg * page
    assert B_BLOCK % page == 0 and past % B_BLOCK == 0 and s_len <= SUBLANES
    own_blk = past // B_BLOCK
    assert own_blk >= 1
    n_valid = min(own_blk, B_TOPK)
    nsel = s_len * B_TOPK

    means = _cache_block_means(cache_k, layer, page_table).reshape(db, own_blk, hdm)
    q_rep = jnp.tile(q, (1, B_HEADS, 1))
    top = _moba_select(q_rep, means, s_len, n_valid)[:, :, :B_TOPK]
    top = jnp.minimum(top, own_blk - 1)
    idx = top.reshape(db, B_HEADS, nsel)
    ppb = B_BLOCK // page
    pages = idx[..., None] * ppb + jnp.arange(ppb, dtype=jnp.int32)
    phys = jnp.take_along_axis(page_table[:, None, :], pages.reshape(db, 1, -1), axis=2)
    phys = phys.reshape(-1).astype(jnp.int32)
    idx_flat = idx.reshape(-1).astype(jnp.int32)

    npages = nsel * ppb
    row_spec = pl.BlockSpec((1, s_len, hd), lambda g, ph, ix: (g // B_HEADS, 0, g % B_HEADS))
    kern = functools.partial(_moba_sample_kernel, layer=layer, s_len=s_len, hd=hd, past=past, n_valid=n_valid,
                             nsel=nsel, ppb=ppb)
    return pl.pallas_call(
        kern,
        grid_spec=pltpu.PrefetchScalarGridSpec(
            num_scalar_prefetch=2,
            grid=(db * B_HEADS,),
            in_specs=[pl.BlockSpec(memory_space=pltpu.SMEM), row_spec, row_spec, row_spec,
                      pl.BlockSpec(memory_space=pl.ANY), pl.BlockSpec(memory_space=pl.ANY)],
            out_specs=row_spec,
            scratch_shapes=[pltpu.VMEM((2, npages, page, hd), F32), pltpu.VMEM((2, npages, page, hd), F32),
                            pltpu.SemaphoreType.DMA((2, 2)), pltpu.VMEM((SUBLANES, hd), F32)],
        ),
        out_shape=jax.ShapeDtypeStruct((db, s_len, hdm), F32),
        compiler_params=_cparams("arbitrary"),
        name="moba_sample_attend",
    )(phys, idx_flat, slopes, q, k_new, v_new, cache_k, cache_v)


def _hgrn_kernel(q_ref, k_ref, i_ref, f_ref, g_ref, ng_ref, s0_ref, y_ref, s_ref, st_ref, gc_ref, o_ref,
                 *, t_step, cs, hps):
    c = pl.program_id(2)
    nsub = cs // SUB
    dk = dv = LANES

    @pl.when(c == 0)
    def _():
        for hh in range(hps):
            st_ref[hh] = s0_ref[0, hh].T

    gc_ref[...] = _tri_cumsum(f_ref[0], cs)
    rsub = lax.broadcasted_iota(jnp.int32, (SUB, 1), 0)

    def chunk(ci, carry):
        r0 = pl.multiple_of(ci * cs, cs)
        for hh in range(hps):
            q = q_ref[0, pl.ds(r0, cs), hh * dk:(hh + 1) * dk]
            k = k_ref[0, pl.ds(r0, cs), hh * dk:(hh + 1) * dk]
            iv = i_ref[0, pl.ds(r0, cs), hh * dv:(hh + 1) * dv]
            gcum = gc_ref[pl.ds(r0, cs), hh * dk:(hh + 1) * dk]
            st = st_ref[hh]
            st16 = st.astype(BF16)
            iv16 = iv.astype(BF16)
            o_inter = _dot_nt((q * jnp.exp(gcum)).astype(BF16), st16)
            for si in range(nsub):
                lo, hi = si * SUB, (si + 1) * SUB
                g_i = gcum[lo:hi]
                q_i = q[lo:hi]
                k_i = k[lo:hi]
                i_i = iv[lo:hi]
                acc = o_inter[lo:hi]
                if si > 0:
                    gref = gcum[lo - 1:lo]
                    qt = (q_i * jnp.exp(g_i - gref)).astype(BF16)
                    kt = (k[:lo] * jnp.exp(gref - gcum[:lo])).astype(BF16)
                    att = _dot_nt(qt, kt)
                    acc = acc + jnp.dot(att.astype(BF16), iv16[:lo], preferred_element_type=F32)
                for s in range(SUB):
                    e = jnp.exp(jnp.where(rsub >= s, g_i - g_i[s:s + 1], NEG_INF))
                    a = jnp.sum(q_i * k_i[s:s + 1] * e, axis=-1, keepdims=True)
                    acc = acc + a * i_i[s:s + 1]
                o_ref[pl.ds(r0 + lo, SUB), hh * dv:(hh + 1) * dv] = acc
            glast = gcum[cs - 1:cs]
            kh = (k * jnp.exp(glast - gcum)).astype(BF16)
            st_ref[hh] = st * jnp.exp(glast) + _dot_tn(iv16, kh)
        return carry

    lax.fori_loop(0, t_step // cs, chunk, 0)
    for hh in range(hps):
        o = o_ref[:, hh * dv:(hh + 1) * dv]
        y = o * lax.rsqrt(jnp.mean(o * o, axis=-1, keepdims=True) + EPS) * ng_ref[...]
        y_ref[0, :, hh * dv:(hh + 1) * dv] = (y * g_ref[0, :, hh * dv:(hh + 1) * dv]).astype(y_ref.dtype)

    @pl.when(c == pl.num_programs(2) - 1)
    def _():
        for hh in range(hps):
            s_ref[0, hh] = st_ref[hh].T


HGRN_HEADS_PER_STEP = 16


def _hgrn_recurrence(q, k, iv, logf, gate, norm_g, s0, t_step, cs):
    b, seq, hdm = q.shape
    dk = C_DK
    dv = hdm // C_HEADS
    hps = HGRN_HEADS_PER_STEP
    assert dk == LANES and dv == LANES and seq % t_step == 0 and t_step % cs == 0 and cs % SUB == 0
    assert C_HEADS % hps == 0
    kern = functools.partial(_hgrn_kernel, t_step=t_step, cs=cs, hps=hps)
    tok = pl.BlockSpec((1, t_step, hps * dv), lambda i, h, c: (i, c, h))
    st = pl.BlockSpec((1, hps, dk, dv), lambda i, h, c: (i, h, 0, 0))
    return pl.pallas_call(
        kern,
        grid=(b, C_HEADS // hps, seq // t_step),
        in_specs=[tok, tok, tok, tok, tok, pl.BlockSpec((1, dv), lambda i, h, c: (0, 0)), st],
        out_specs=[tok, st],
        out_shape=[jax.ShapeDtypeStruct((b, seq, hdm), BF16),
                   jax.ShapeDtypeStruct((b, C_HEADS, dk, dv), F32)],
        scratch_shapes=[pltpu.VMEM((hps, dv, dk), F32), pltpu.VMEM((t_step, hps * dk), F32),
                        pltpu.VMEM((t_step, hps * dv), F32)],
        compiler_params=_cparams("parallel", "parallel", "arbitrary"),
        name="hgrn_recurrence",
    )(q, k, iv, logf, gate, norm_g.reshape(1, dv), s0)


def _mlstm_kernel(q_ref, k_ref, v_ref, og_ref, gt_ref, ng_ref, c0_ref, n0_ref, m0_ref,
                  y_ref, c_ref, n_ref, m_ref, *, c, dk, dv, valid_len):
    @pl.when(pl.program_id(1) == 0)
    def _():
        c_ref[...] = c0_ref[...]
        n_ref[...] = n0_ref[...]
        m_ref[...] = m0_ref[...]

    gt = gt_ref[0]
    rowc = lax.broadcasted_iota(jnp.int32, (c, LANES), 0)
    lanec = lax.broadcasted_iota(jnp.int32, (c, LANES), 1)
    gt = jnp.where(rowc < valid_len, gt, jnp.where(lanec < D_HEADS, -1e30, 0.0))
    cum = _tri_cumsum(gt, c)
    gt_t = gt.T
    cum_t = cum.T
    r_i = lax.broadcasted_iota(jnp.int32, (c, c), 0)
    c_i = lax.broadcasted_iota(jnp.int32, (c, c), 1)
    tril = c_i <= r_i
    for h in range(D_HEADS):
        b_col = cum[:, D_HEADS + h:D_HEADS + h + 1]
        b_row = cum_t[D_HEADS + h:D_HEADS + h + 1, :]
        li_col = gt[:, h:h + 1]
        li_row = gt_t[h:h + 1, :]
        m_prev = m_ref[0, h][:, 0:1]
        n_row = n_ref[0, h]
        c_st = c_ref[0, h]
        qh = q_ref[0, :, h * dk:(h + 1) * dk]
        kh = k_ref[0, :, h * dk:(h + 1) * dk]
        vh16 = v_ref[0, :, h * dv:(h + 1) * dv].astype(BF16)
        qh16 = qh.astype(BF16)

        dmat = jnp.where(tril, b_col - b_row + li_row, NEG_INF)
        inter = b_col + m_prev
        m_t = jnp.maximum(inter, jnp.max(dmat, axis=-1, keepdims=True))
        w_inter = jnp.exp(inter - m_t)
        wqk = jnp.exp(dmat - m_t) * _dot_nt(qh16, kh.astype(BF16))
        num = w_inter * jnp.dot(qh16, c_st.astype(BF16), preferred_element_type=F32) \
            + jnp.dot(wqk.astype(BF16), vh16, preferred_element_type=F32)
        den = w_inter * jnp.sum(qh * n_row, axis=-1, keepdims=True) + jnp.sum(wqk, axis=-1, keepdims=True)
        hh = num / jnp.maximum(jnp.abs(den), jnp.exp(-m_t))

        b_last = b_col[c - 1:c, :]
        a_col = b_last - b_col + li_col
        m_new = jnp.maximum(b_last + m_prev, jnp.max(a_col, axis=0, keepdims=True))
        w_c = jnp.exp(b_last + m_prev - m_new)
        w_s = jnp.exp(a_col - m_new)
        ks = w_s * kh
        c_ref[0, h] = w_c * c_st + _dot_tn(ks.astype(BF16), vh16)
        n_ref[0, h] = w_c * n_row + jnp.sum(ks, axis=0, keepdims=True)
        m_ref[0, h] = jnp.broadcast_to(m_new, (1, LANES))

        y = hh * lax.rsqrt(jnp.mean(hh * hh, axis=-1, keepdims=True) + EPS) * ng_ref[:, h * dv:(h + 1) * dv]
        y_ref[0, :, h * dv:(h + 1) * dv] = (y * og_ref[0, :, h * dv:(h + 1) * dv]).astype(y_ref.dtype)


def _mlstm_recurrence(q, k, v, og, gates, norm_g, c0, n0, m0, c, valid_len):
    b, seq, _ = q.shape
    dk = q.shape[2] // D_HEADS
    dv = v.shape[2] // D_HEADS
    assert seq % c == 0 and (seq == c or valid_len == c)
    n0 = n0.reshape(b, D_HEADS, 1, dk)
    m0 = jnp.broadcast_to(m0.reshape(b, D_HEADS, 1, 1), (b, D_HEADS, 1, LANES))
    kern = functools.partial(_mlstm_kernel, c=c, dk=dk, dv=dv, valid_len=valid_len)

    def tok(w):
        return pl.BlockSpec((1, c, w), lambda i, j: (i, j, 0))

    c_spec = pl.BlockSpec((1, D_HEADS, dk, dv), lambda i, j: (i, 0, 0, 0))
    n_spec = pl.BlockSpec((1, D_HEADS, 1, dk), lambda i, j: (i, 0, 0, 0))
    m_spec = pl.BlockSpec((1, D_HEADS, 1, LANES), lambda i, j: (i, 0, 0, 0))
    y, c_out, n_out, m_out = pl.pallas_call(
        kern,
        grid=(b, seq // c),
        in_specs=[tok(D_HEADS * dk), tok(D_HEADS * dk), tok(D_HEADS * dv), tok(D_HEADS * dv), tok(LANES),
                  pl.BlockSpec((1, D_HEADS * dv), lambda i, j: (0, 0)), c_spec, n_spec, m_spec],
        out_specs=[tok(D_HEADS * dv), c_spec, n_spec, m_spec],
        out_shape=[jax.ShapeDtypeStruct((b, seq, D_HEADS * dv), BF16),
                   jax.ShapeDtypeStruct((b, D_HEADS, dk, dv), F32),
                   jax.ShapeDtypeStruct((b, D_HEADS, 1, dk), F32),
                   jax.ShapeDtypeStruct((b, D_HEADS, 1, LANES), F32)],
        compiler_params=_cparams("parallel", "arbitrary"),
        name="mlstm_recurrence",
    )(q, k, v, og, gates, norm_g.reshape(1, D_HEADS * dv), c0, n0, m0)
    return y, c_out, n_out.reshape(b, D_HEADS, dk), m_out[:, :, 0, 0]


def _ident(z):
    return (z,)


def _gelu(z):
    return (jax.nn.gelu(z, approximate=True),)


def _mix_gmlp(x, g, j, w_in, ln_g, ln_b, w_s, b_s, w_out):
    b, seq, d = x.shape
    width = w_in.shape[2] // 2
    chunk = w_s.shape[1]
    c = chunk if seq % chunk == 0 else seq
    x2 = x.reshape(b * seq, d)
    (z,) = _norm_matmul(x2, g, w_in, j, 0, 2 * width, _gelu, (F32,))
    a, v = _gmlp_gate(z.reshape(b * seq // c, c, 2 * width), ln_g, ln_b, w_s, b_s, c)
    y = _matmul_residual(a.reshape(b * seq, width), w_out, j, x2)
    return y.reshape(b, seq, d), v.reshape(b, seq, width)


def _moba_qkv(x, g, j, w_qkv):
    b, seq, d = x.shape
    hdm = w_qkv.shape[2] // 3
    x2 = x.reshape(b * seq, d)
    qkv = _norm_matmul_parts(x2, g, w_qkv, j, 0, [(hdm, _ident, (F32,), ())] * 3)
    return [p[0].reshape(b, seq, hdm) for p in qkv]


def _hgrn_project(x2, g, j, w_in, lb):
    hdm = w_in.shape[2] // 4
    lb = lb.reshape(1, hdm)

    def silu(z):
        return (z * _sigmoid(z),)

    def forget(z, lbv):
        logf = jnp.log(lbv + (1.0 - lbv) * _sigmoid(z))
        return logf, (1.0 - lbv) * _sigmoid(-z)

    (q,), (logf, k), (iv,), (gate,) = _norm_matmul_parts(
        x2, g, w_in, j, 0, [(hdm, silu, (F32,), ()), (hdm, forget, (F32, F32), (lb,)),
                            (hdm, _ident, (F32,), ()), (hdm, silu, (F32,), ())])
    return q, k, iv, logf, gate


def _mix_hgrn(x, g, j, w_in, lb, norm_g, w_out, s0):
    b, seq, d = x.shape
    x2 = x.reshape(b * seq, d)
    parts = [p.reshape(b, seq, -1) for p in _hgrn_project(x2, g, j, w_in, lb)]
    if seq % 256 == 0:
        t_step, cs, pad = 256, 64, 0
    else:
        t_step = cs = -(-seq // SEQ_PAD) * SEQ_PAD
        pad = t_step - seq
        parts = [jnp.pad(p, ((0, 0), (0, pad), (0, 0))) for p in parts]
    y, s = _hgrn_recurrence(*parts, norm_g, s0, t_step, cs)
    y = y[:, :seq].reshape(b * seq, -1)
    return _matmul_residual(y, w_out, j, x2).reshape(b, seq, d), s


def _mix_mlstm(x, g, j, w_in, w_gates, b_gates, norm_g, w_out, c0, n0, m0):
    b, seq, d = x.shape
    dk = c0.shape[2]
    dv = c0.shape[3]
    nq, nv = D_HEADS * dk, D_HEADS * dv
    x2 = x.reshape(b * seq, d)

    def kscale(z):
        return (z * (dk ** -0.5),)

    def ogate(z):
        return (_sigmoid(z),)

    def gates_fn(z, bias):
        zz = z + bias
        lane = lax.broadcasted_iota(jnp.int32, zz.shape, 1)
        return (jnp.where(lane < D_HEADS, zz, _log_sigmoid(zz)),)

    (q,), (k,), (v,), (og,) = _norm_matmul_parts(
        x2, g, w_in, j, 0, [(nq, _ident, (BF16,), ()), (nq, kscale, (BF16,), ()),
                            (nv, _ident, (BF16,), ()), (nv, ogate, (BF16,), ())])
    (gt,) = _norm_matmul(x2, g, w_gates, 0, 0, LANES, gates_fn, (F32,), extras=(b_gates,))
    parts = [p.reshape(b, seq, -1) for p in (q, k, v, og, gt)]
    if seq % 256 == 0:
        c, valid = 256, 256
    else:
        c = -(-seq // SEQ_PAD) * SEQ_PAD
        valid = seq
        parts = [jnp.pad(p, ((0, 0), (0, c - seq), (0, 0))) for p in parts]
    y, c_out, n_out, m_out = _mlstm_recurrence(*parts, norm_g, c0, n0, m0, c, valid)
    y = y[:, :seq].reshape(b * seq, nv)
    return _matmul_residual(y, w_out, j, x2).reshape(b, seq, d), c_out, n_out, m_out


def kernel(x_prompt, x_sample, cache_k, cache_v, page_table, state_hgrn, state_mlstm_c, state_mlstm_n, state_mlstm_m, norm_mix, norm_ffn, norm_final, w_ffn_up, w_ffn_down, a_w_in, a_ln_g, a_ln_b, a_w_s, a_b_s, a_w_out, b_w_qkv, b_w_out, c_w_in, c_lower_bound, c_norm_g, c_w_out, d_w_in, d_b_gates, d_norm_g, d_w_out):
    depth = norm_mix.shape[0]
    bsz, seq, d = x_prompt.shape
    dbsz, dseq, _ = x_sample.shape
    slopes = jnp.asarray(2.0 ** (-8.0 * np.arange(1, B_HEADS + 1) / B_HEADS), F32)
    lbs = jax.nn.softmax(c_lower_bound.astype(F32), axis=0)
    lbs = jnp.cumsum(lbs, axis=0) - lbs[0]
    hdm = b_w_qkv.shape[2] // 3

    xp, xs = x_prompt, x_sample
    outs = {k: [] for k in ("av", "kp", "vp", "ks", "vs", "hp", "hs", "cp", "np", "mp", "cs", "ns", "ms")}
    for layer in range(depth):
        kind = layer % N_MIXERS
        j = layer // N_MIXERS
        g = norm_mix[layer]
        if kind == 0:
            args = (j, a_w_in, a_ln_g[j], a_ln_b[j], a_w_s[j], a_b_s[j], a_w_out)
            xp, _ = _mix_gmlp(xp, g, *args)
            xs, vrow = _mix_gmlp(xs, g, *args)
            outs["av"].append(vrow)
        elif kind == 1:
            qp, kp, vp = _moba_qkv(xp, g, j, b_w_qkv)
            op = _moba_prompt(qp, kp, vp, slopes)
            xp = _matmul_residual(op.reshape(bsz * seq, hdm), b_w_out, j,
                                  xp.reshape(bsz * seq, d)).reshape(bsz, seq, d)
            qs, ks, vs = _moba_qkv(xs, g, j, b_w_qkv)
            os_ = _moba_sample(qs, ks, vs, cache_k, cache_v, j, page_table, slopes)
            xs = _matmul_residual(os_.reshape(dbsz * dseq, hdm), b_w_out, j,
                                  xs.reshape(dbsz * dseq, d)).reshape(dbsz, dseq, d)
            hd = hdm // B_HEADS
            outs["kp"].append(kp.reshape(bsz, seq, B_HEADS, hd))
            outs["vp"].append(vp.reshape(bsz, seq, B_HEADS, hd))
            outs["ks"].append(ks.reshape(dbsz, dseq, B_HEADS, hd))
            outs["vs"].append(vs.reshape(dbsz, dseq, B_HEADS, hd))
        elif kind == 2:
            args = (j, c_w_in, lbs[layer], c_norm_g[j], c_w_out)
            s0 = jnp.zeros((bsz,) + state_hgrn.shape[2:], F32)
            xp, sp = _mix_hgrn(xp, g, *args, s0)
            xs, ss = _mix_hgrn(xs, g, *args, state_hgrn[j])
            outs["hp"].append(sp)
            outs["hs"].append(ss)
        else:
            ng = 2 * D_HEADS * (state_mlstm_c.shape[3] + state_mlstm_c.shape[4])
            w_gates = jnp.pad(d_w_in[j][:, ng:], ((0, 0), (0, LANES - 2 * D_HEADS)))[None]
            b_gates = jnp.pad(d_b_gates[j], (0, LANES - 2 * D_HEADS)).reshape(1, LANES)
            args = (j, d_w_in, w_gates, b_gates, d_norm_g[j], d_w_out)
            zc = jnp.zeros((bsz,) + state_mlstm_c.shape[2:], F32)
            zn = jnp.zeros((bsz,) + state_mlstm_n.shape[2:], F32)
            zm = jnp.zeros((bsz,) + state_mlstm_m.shape[2:], F32)
            xp, cp, np_, mp = _mix_mlstm(xp, g, *args, zc, zn, zm)
            xs, cs, ns, ms = _mix_mlstm(xs, g, *args, state_mlstm_c[j], state_mlstm_n[j], state_mlstm_m[j])
            for key, val in zip(("cp", "np", "mp", "cs", "ns", "ms"), (cp, np_, mp, cs, ns, ms)):
                outs[key].append(val)
        xp = _ffn(xp.reshape(bsz * seq, d), norm_ffn[layer], w_ffn_up, w_ffn_down, layer).reshape(bsz, seq, d)
        xs = _ffn(xs.reshape(dbsz * dseq, d), norm_ffn[layer], w_ffn_up, w_ffn_down, layer).reshape(dbsz, dseq, d)
    y_prompt = _rmsnorm(xp.reshape(bsz * seq, d), norm_final).reshape(bsz, seq, d)
    y_sample = _rmsnorm(xs.reshape(dbsz * dseq, d), norm_final).reshape(dbsz, dseq, d)
    st = {k: jnp.stack(v) for k, v in outs.items()}
    return (y_prompt, y_sample, st["av"], st["kp"], st["vp"], st["ks"], st["vs"], st["hp"], st["hs"],
            st["cp"], st["np"], st["mp"], st["cs"], st["ns"], st["ms"])
```
